```python
import math
import jax, jax.numpy as jnp
from jax import lax
import numpy as np

D_MODEL = 2048
BATCH = 4
SEQ = 2048
DEPTH = 4
DEC_BATCH = 128
DEC_SEQ = 8
PAST_LEN = 16384
PAGE_SIZE = 128

POOL_WINDOWS = (2, 4, 8, 16)
N_POOL_GROUPS = 4
POOL_DIM = D_MODEL // 2
POOL_GROUP_DIM = POOL_DIM // N_POOL_GROUPS
POOL_BUF = max(POOL_WINDOWS) - 1
SSD_INNER = D_MODEL
SSD_HEAD_DIM = 64
SSD_HEADS = SSD_INNER // SSD_HEAD_DIM
SSD_GROUPS = 4
HEADS_PER_GROUP = SSD_HEADS // SSD_GROUPS
D_STATE = 128
CONV_W = 4
CONV_DIM = SSD_INNER + 2 * SSD_GROUPS * D_STATE
CHUNK = 128
N_EXPERT_GROUPS = 4
EXPERTS_PER_GROUP = 8
N_EXPERTS = N_EXPERT_GROUPS * EXPERTS_PER_GROUP
TOP_K_IN_GROUP = 2
D_EXPERT = D_MODEL // 4
DN_ALPHA = (2.0 * DEPTH) ** 0.25
DN_BETA = (8.0 * DEPTH) ** -0.25
LN_EPS = 1e-5
RMS_EPS = 1e-5
IN_SPLITS = (POOL_DIM, SSD_INNER, CONV_DIM, SSD_HEADS, D_MODEL, D_MODEL)
IN_DIM = sum(IN_SPLITS)
SPLIT_POINTS = tuple(int(v) for v in np.cumsum(IN_SPLITS)[:-1])

kernel_name = 'hybrid_pool_ssd_hmoe_deepnorm_step'


def layer_norm(x, g, b):
    xf = x.astype(jnp.float32)
    mu = jnp.mean(xf, -1, keepdims=True)
    var = jnp.mean(jnp.square(xf - mu), -1, keepdims=True)
    return ((xf - mu) * lax.rsqrt(var + LN_EPS) * g + b).astype(x.dtype)


def gated_rmsnorm(y, z, w):
    v = (y * jax.nn.silu(z)).astype(jnp.float32)
    shp = v.shape
    vg = v.reshape(shp[:-1] + (SSD_GROUPS, shp[-1] // SSD_GROUPS))
    vg = vg * lax.rsqrt(jnp.mean(jnp.square(vg), -1, keepdims=True) + RMS_EPS)
    return (vg.reshape(shp) * w).astype(y.dtype)


def multiscale_pool(u, buf, pos0, map_w, map_b, scale):
    bsz, t_len, _ = u.shape
    ext = jnp.concatenate([buf.astype(u.dtype), u], axis=1)
    cs = jnp.cumsum(ext.astype(jnp.float32), axis=1)
    cs = jnp.concatenate([jnp.zeros_like(cs[:, :1]), cs], axis=1)
    end = cs[:, POOL_BUF + 1:]
    pos = pos0 + jnp.arange(t_len)
    pooled = []
    for gi, win in enumerate(POOL_WINDOWS):
        ch = slice(gi * POOL_GROUP_DIM, (gi + 1) * POOL_GROUP_DIM)
        start = cs[:, POOL_BUF + 1 - win: POOL_BUF + 1 - win + t_len, ch]
        cnt = jnp.minimum(pos + 1, win).astype(jnp.float32)[None, :, None]
        pooled.append((end[:, :, ch] - start) / cnt)
    d = jnp.concatenate(pooled, -1).astype(u.dtype) - u
    d = d.reshape(bsz, t_len, N_POOL_GROUPS, POOL_GROUP_DIM)
    mixed = jnp.einsum('btgc,gcd->btgd', d, map_w) + map_b
    out = mixed.reshape(bsz, t_len, POOL_DIM) * scale
    return out, ext[:, -POOL_BUF:]


def causal_dwconv(xbc, buf, w, bias):
    t_len = xbc.shape[1]
    ext = jnp.concatenate([buf.astype(xbc.dtype), xbc], axis=1)
    out = ext[:, 0:t_len] * w[0]
    for k in range(1, CONV_W):
        out = out + ext[:, k:k + t_len] * w[k]
    return jax.nn.silu(out + bias), ext[:, -(CONV_W - 1):]


def ssd_chunked(xh, dt, a, bm, cm, h0):
    bsz, t_len = xh.shape[:2]
    q = min(CHUNK, t_len)
    nc = t_len // q
    x = xh.reshape(bsz, nc, q, SSD_GROUPS, HEADS_PER_GROUP, SSD_HEAD_DIM)
    dtc = dt.reshape(bsz, nc, q, SSD_GROUPS, HEADS_PER_GROUP)
    bc = bm.reshape(bsz, nc, q, SSD_GROUPS, D_STATE)
    cc = cm.reshape(bsz, nc, q, SSD_GROUPS, D_STATE)
    acum = jnp.cumsum(dtc * a.reshape(SSD_GROUPS, HEADS_PER_GROUP), axis=2)
    seg = acum[:, :, :, None] - acum[:, :, None, :]
    causal = jnp.tril(jnp.ones((q, q), dtype=bool))[:, :, None, None]
    decay = jnp.exp(jnp.where(causal, seg, -jnp.inf))
    cb = jnp.einsum('bclgn,bcsgn->bclsg', cc, bc)
    y_diag = jnp.einsum('bclsgk,bcsgkp->bclgkp', cb[..., None] * decay * dtc[:, :, None], x)
    decay_to_end = jnp.exp(acum[:, :, -1:] - acum)
    states = jnp.einsum('bclgn,bclgk,bclgkp->bcgkpn', bc, decay_to_end * dtc, x)
    chunk_decay = jnp.exp(acum[:, :, -1])

    def step(h, inp):
        s, dcy = inp
        return h * dcy[..., None, None] + s, h

    h_init = h0.reshape(bsz, SSD_GROUPS, HEADS_PER_GROUP, SSD_HEAD_DIM, D_STATE).astype(states.dtype)
    h_last, h_prev = lax.scan(step, h_init, (jnp.moveaxis(states, 1, 0), jnp.moveaxis(chunk_decay, 1, 0)))
    h_prev = jnp.moveaxis(h_prev, 0, 1)
    y_off = jnp.einsum('bclgn,bcgkpn,bclgk->bclgkp', cc, h_prev, jnp.exp(acum))
    y = (y_diag + y_off).reshape(bsz, t_len, SSD_HEADS, SSD_HEAD_DIM)
    return y, h_last.reshape(bsz, SSD_HEADS, SSD_HEAD_DIM, D_STATE)


def hier_moe(x, router_group_w, router_group_b, router_expert_w, router_expert_b,
             expert_w_gate, expert_w_up, expert_w_down):
    bsz, t_len, dm = x.shape
    t = x.reshape(-1, dm)
    n = t.shape[0]
    g_logits = (t @ router_group_w + router_group_b).astype(jnp.float32)
    g_prob = jax.nn.softmax(g_logits, axis=-1)
    g_val, g_idx = lax.top_k(g_prob, 1)
    e_logits = (t @ router_expert_w + router_expert_b).astype(jnp.float32)
    e_logits = e_logits.reshape(n, N_EXPERT_GROUPS, EXPERTS_PER_GROUP)
    e_in = jnp.take_along_axis(e_logits, g_idx[:, :, None], axis=1)[:, 0]
    e_prob = jax.nn.softmax(e_in, axis=-1)
    top_v, top_i = lax.top_k(e_prob, TOP_K_IN_GROUP)
    top_v = top_v / jnp.sum(top_v, -1, keepdims=True)
    w_grp = jnp.sum(jax.nn.one_hot(top_i, EXPERTS_PER_GROUP) * top_v[..., None], axis=1) * g_val
    combine = (jax.nn.one_hot(g_idx[:, 0], N_EXPERT_GROUPS)[:, :, None] * w_grp[:, None, :]).astype(x.dtype)
    y = jnp.zeros_like(t)
    for gi in range(N_EXPERT_GROUPS):
        h = jax.nn.silu(jnp.einsum('nd,edf->nef', t, expert_w_gate[gi])) * jnp.einsum('nd,edf->nef', t, expert_w_up[gi])
        y = y + jnp.einsum('nef,efd->nd', h * combine[:, gi, :, None], expert_w_down[gi])
    return y.reshape(bsz, t_len, dm)


def decoder_layer(x, ssm0, conv_buf, pool_buf, pos0,
                  w_in, pool_map_w, pool_map_b, pool_scale, conv_w, conv_b,
                  dt_bias, a_log, d_skip, ssd_norm_w, gate_a_bias, gate_b_bias,
                  w_pool_out, w_ssd_out, w_o, ln1_g, ln1_b,
                  router_group_w, router_group_b, router_expert_w, router_expert_b,
                  expert_w_gate, expert_w_up, expert_w_down, ln2_g, ln2_b):
    bsz, t_len, _ = x.shape
    proj = x @ w_in
    u, z, xbc, dt_raw, ga, gb = jnp.split(proj, SPLIT_POINTS, axis=-1)
    a_out, new_pool = multiscale_pool(u, pool_buf, pos0, pool_map_w, pool_map_b, pool_scale)
    xbc, new_conv = causal_dwconv(xbc, conv_buf, conv_w, conv_b)
    xs, bm, cm = jnp.split(xbc, (SSD_INNER, SSD_INNER + SSD_GROUPS * D_STATE), axis=-1)
    dt = jax.nn.softplus(dt_raw + dt_bias)
    a = -jnp.exp(a_log)
    xh = xs.reshape(bsz, t_len, SSD_HEADS, SSD_HEAD_DIM)
    y, new_ssm = ssd_chunked(xh, dt, a,
                             bm.reshape(bsz, t_len, SSD_GROUPS, D_STATE),
                             cm.reshape(bsz, t_len, SSD_GROUPS, D_STATE), ssm0)
    y = (y + xh * d_skip[:, None]).reshape(bsz, t_len, SSD_INNER)
    y = gated_rmsnorm(y, z, ssd_norm_w)
    m = jax.nn.sigmoid(ga + gate_a_bias) * (a_out @ w_pool_out) + jax.nn.sigmoid(gb + gate_b_bias) * (y @ w_ssd_out)
    x = layer_norm(DN_ALPHA * x + m @ w_o, ln1_g, ln1_b)
    f = hier_moe(x, router_group_w, router_group_b, router_expert_w, router_expert_b,
                 expert_w_gate, expert_w_up, expert_w_down)
    x = layer_norm(DN_ALPHA * x + f, ln2_g, ln2_b)
    return x, new_ssm, new_conv, new_pool


def setup_inputs(seed: int = 0) -> dict:
    key = jax.random.key(seed)
    ks = jax.random.split(key, 32)
    f32 = jnp.float32

    def nrm(k, shape, scale):
        return jax.random.normal(k, shape, f32) * scale

    dt0 = jnp.exp(jax.random.uniform(ks[10], (DEPTH, SSD_HEADS), f32) * (math.log(0.1) - math.log(0.001)) + math.log(0.001))
    return {
        'x_prompt': nrm(ks[0], (BATCH, SEQ, D_MODEL), 1.0),
        'x_sample': nrm(ks[1], (DEC_BATCH, DEC_SEQ, D_MODEL), 1.0),
        'state_ssm': nrm(ks[2], (DEPTH, DEC_BATCH, SSD_HEADS, SSD_HEAD_DIM, D_STATE), 0.1),
        'state_conv': nrm(ks[3], (DEPTH, DEC_BATCH, CONV_W - 1, CONV_DIM), 1.0),
        'state_pool': nrm(ks[4], (DEPTH, DEC_BATCH, POOL_BUF, POOL_DIM), 1.0),
        'w_in': nrm(ks[5], (DEPTH, D_MODEL, IN_DIM), D_MODEL ** -0.5),
        'pool_map_w': nrm(ks[6], (DEPTH, N_POOL_GROUPS, POOL_GROUP_DIM, POOL_GROUP_DIM), POOL_GROUP_DIM ** -0.5),
        'pool_map_b': nrm(ks[7], (DEPTH, N_POOL_GROUPS, POOL_GROUP_DIM), 0.02),
        'pool_scale': 1.0 + nrm(ks[8], (DEPTH, POOL_DIM), 0.02),
        'conv_w': nrm(ks[9], (DEPTH, CONV_W, CONV_DIM), CONV_W ** -0.5),
        'conv_b': nrm(ks[11], (DEPTH, CONV_DIM), 0.02),
        'dt_bias': dt0 + jnp.log(-jnp.expm1(-dt0)),
        'a_log': jnp.log(jax.random.uniform(ks[12], (DEPTH, SSD_HEADS), f32, 1.0, 16.0)),
        'd_skip': 1.0 + nrm(ks[13], (DEPTH, SSD_HEADS), 0.1),
        'ssd_norm_w': 1.0 + nrm(ks[14], (DEPTH, SSD_INNER), 0.02),
        'gate_a_bias': nrm(ks[15], (DEPTH, D_MODEL), 0.02),
        'gate_b_bias': nrm(ks[16], (DEPTH, D_MODEL), 0.02),
        'w_pool_out': nrm(ks[17], (DEPTH, POOL_DIM, D_MODEL), DN_BETA * POOL_DIM ** -0.5),
        'w_ssd_out': nrm(ks[18], (DEPTH, SSD_INNER, D_MODEL), DN_BETA * SSD_INNER ** -0.5),
        'w_o': nrm(ks[19], (DEPTH, D_MODEL, D_MODEL), DN_BETA * D_MODEL ** -0.5),
        'ln1_g': 1.0 + nrm(ks[20], (DEPTH, D_MODEL), 0.02),
        'ln1_b': nrm(ks[21], (DEPTH, D_MODEL), 0.02),
        'router_group_w': nrm(ks[22], (DEPTH, D_MODEL, N_EXPERT_GROUPS), D_MODEL ** -0.5),
        'router_group_b': nrm(ks[23], (DEPTH, N_EXPERT_GROUPS), 0.01),
        'router_expert_w': nrm(ks[24], (DEPTH, D_MODEL, N_EXPERTS), D_MODEL ** -0.5),
        'router_expert_b': nrm(ks[25], (DEPTH, N_EXPERTS), 0.01),
        'expert_w_gate': nrm(ks[26], (DEPTH, N_EXPERT_GROUPS, EXPERTS_PER_GROUP, D_MODEL, D_EXPERT), D_MODEL ** -0.5),
        'expert_w_up': nrm(ks[27], (DEPTH, N_EXPERT_GROUPS, EXPERTS_PER_GROUP, D_MODEL, D_EXPERT), DN_BETA * D_MODEL ** -0.5),
        'expert_w_down': nrm(ks[28], (DEPTH, N_EXPERT_GROUPS, EXPERTS_PER_GROUP, D_EXPERT, D_MODEL), DN_BETA * D_EXPERT ** -0.5),
        'ln2_g': 1.0 + nrm(ks[29], (DEPTH, D_MODEL), 0.02),
        'ln2_b': nrm(ks[30], (DEPTH, D_MODEL), 0.02),
    }


def reference(x_prompt, x_sample, state_ssm, state_conv, state_pool,
              w_in, pool_map_w, pool_map_b, pool_scale, conv_w, conv_b,
              dt_bias, a_log, d_skip, ssd_norm_w, gate_a_bias, gate_b_bias,
              w_pool_out, w_ssd_out, w_o, ln1_g, ln1_b,
              router_group_w, router_group_b, router_expert_w, router_expert_b,
              expert_w_gate, expert_w_up, expert_w_down, ln2_g, ln2_b):
    xp = x_prompt
    xsm = x_sample
    bp = x_prompt.shape[0]
    zero_ssm = jnp.zeros((bp, SSD_HEADS, SSD_HEAD_DIM, D_STATE), x_prompt.dtype)
    zero_conv = jnp.zeros((bp, CONV_W - 1, CONV_DIM), x_prompt.dtype)
    zero_pool = jnp.zeros((bp, POOL_BUF, POOL_DIM), x_prompt.dtype)
    ssm_p, conv_p, pool_p = [], [], []
    ssm_s, conv_s, pool_s = [], [], []
    for l in range(DEPTH):
        lw = (w_in[l], pool_map_w[l], pool_map_b[l], pool_scale[l], conv_w[l], conv_b[l],
              dt_bias[l], a_log[l], d_skip[l], ssd_norm_w[l], gate_a_bias[l], gate_b_bias[l],
              w_pool_out[l], w_ssd_out[l], w_o[l], ln1_g[l], ln1_b[l],
              router_group_w[l], router_group_b[l], router_expert_w[l], router_expert_b[l],
              expert_w_gate[l], expert_w_up[l], expert_w_down[l], ln2_g[l], ln2_b[l])
        xp, h_p, c_p, p_p = decoder_layer(xp, zero_ssm, zero_conv, zero_pool, 0, *lw)
        xsm, h_s, c_s, p_s = decoder_layer(xsm, state_ssm[l], state_conv[l], state_pool[l], PAST_LEN, *lw)
        ssm_p.append(h_p); conv_p.append(c_p); pool_p.append(p_p)
        ssm_s.append(h_s); conv_s.append(c_s); pool_s.append(p_s)
    new_ssm_prompt = jnp.stack(ssm_p, 0)
    new_conv_prompt = jnp.stack(conv_p, 0)
    new_pool_prompt = jnp.stack(pool_p, 0)
    new_ssm_sample = jnp.stack(ssm_s, 0)
    new_conv_sample = jnp.stack(conv_s, 0)
    new_pool_sample = jnp.stack(pool_s, 0)
    return (xp, xsm, new_ssm_prompt, new_conv_prompt, new_pool_prompt, new_ssm_sample, new_conv_sample, new_pool_sample)
```

```python
import functools

import jax
import jax.numpy as jnp
from jax import lax
from jax.experimental import pallas as pl
from jax.experimental.pallas import tpu as pltpu

F32 = jnp.float32
BF = jnp.bfloat16

D_MODEL = 2048
BATCH = 4
SEQ = 2048
DEPTH = 4
DEC_BATCH = 128
DEC_SEQ = 8
PAST_LEN = 16384
POOL_WINDOWS = (2, 4, 8, 16)
N_POOL_GROUPS = 4
POOL_DIM = D_MODEL // 2
POOL_GROUP_DIM = POOL_DIM // N_POOL_GROUPS
POOL_BUF = max(POOL_WINDOWS) - 1
SSD_INNER = D_MODEL
SSD_HEAD_DIM = 64
SSD_HEADS = SSD_INNER // SSD_HEAD_DIM
SSD_GROUPS = 4
HEADS_PER_GROUP = SSD_HEADS // SSD_GROUPS
GROUP_INNER = SSD_INNER // SSD_GROUPS
D_STATE = 128
CONV_W = 4
CONV_DIM = SSD_INNER + 2 * SSD_GROUPS * D_STATE
CHUNK = 128
N_EXPERT_GROUPS = 4
EXPERTS_PER_GROUP = 8
N_EXPERTS = N_EXPERT_GROUPS * EXPERTS_PER_GROUP
D_EXPERT = D_MODEL // 4
DN_ALPHA = (2.0 * DEPTH) ** 0.25
LN_EPS = 1e-5
RMS_EPS = 1e-5
IN_SPLITS = (POOL_DIM, SSD_INNER, CONV_DIM, SSD_HEADS, D_MODEL, D_MODEL)
SPLIT_POINTS = (1024, 3072, 6144, 6176, 8224)

NP = BATCH * SEQ
NS = DEC_BATCH * DEC_SEQ
NTOK = NP + NS

LANES = 128
SUBLANES = 8
VMEM_LIMIT = 56 * 1024 * 1024

COL_Z = 0
COL_GA = 2048
COL_GB = 4096
COL_XBC = 6144
COL_U = 9216
COL_DT = 10240
PROJ_W = COL_DT + LANES

NEG = -1e30

PROJ_TM = 1024
PROJ_TN = 1152
POOL_TM = 512
POOL_SB = 16
POOL_SLAB = 24
SSD_SB = 4
MIX_TM = 256
MOE_T = 256
MOE_TILES = 2 * NTOK // MOE_T + N_EXPERTS
MOE_ROWS = MOE_TILES * MOE_T
DISP_TD = 512
COMB_TC = 256


def _cparams(sem, vmem=VMEM_LIMIT):
    return pltpu.CompilerParams(dimension_semantics=sem, vmem_limit_bytes=vmem)


def _dot(a, b):
    return jnp.dot(a, b, preferred_element_type=F32)


def _dot_nt(a, b):
    return lax.dot_general(a, b, (((1,), (1,)), ((), ())), preferred_element_type=F32)


def _dot_tn(a, b):
    return lax.dot_general(a, b, (((0,), (0,)), ((), ())), preferred_element_type=F32)


def _split2(v):
    hi = v.astype(BF)
    lo = (v - hi.astype(F32)).astype(BF)
    return hi, lo


def _split3(v):
    h1 = v.astype(BF)
    r1 = v - h1.astype(F32)
    h2 = r1.astype(BF)
    h3 = (r1 - h2.astype(F32)).astype(BF)
    return h1, h2, h3


def _expand(v, e_bf):
    hi, lo = _split2(v)
    return _dot(hi, e_bf) + _dot(lo, e_bf)


def _silu(x):
    return x * jax.nn.sigmoid(x)


def _softplus(x):
    return jnp.maximum(x, 0.0) + jnp.log1p(jnp.exp(-jnp.abs(x)))


def _layer_norm(x, g, b):
    mu = jnp.mean(x, axis=-1, keepdims=True)
    xc = x - mu
    var = jnp.mean(xc * xc, axis=-1, keepdims=True)
    return xc * lax.rsqrt(var + LN_EPS) * g + b


def _gated_rmsnorm(y, z, nw):
    v = y * _silu(z)
    outs = []
    for g in range(SSD_GROUPS):
        vg = v[:, g * GROUP_INNER:(g + 1) * GROUP_INNER]
        ms = jnp.mean(vg * vg, axis=-1, keepdims=True)
        outs.append(vg * lax.rsqrt(ms + RMS_EPS))
    return jnp.concatenate(outs, axis=-1) * nw


def _inproj_kernel(x_ref, w_ref, o_ref):
    o_ref[...] = _dot(x_ref[...], w_ref[...])


def _inproj(x_bf, wp, layer):
    return pl.pallas_call(
        _inproj_kernel,
        grid=(PROJ_W // PROJ_TN, NTOK // PROJ_TM),
        in_specs=[
            pl.BlockSpec((PROJ_TM, D_MODEL), lambda j, i: (i, 0)),
            pl.BlockSpec((None, D_MODEL, PROJ_TN), lambda j, i: (layer, 0, j)),
        ],
        out_specs=pl.BlockSpec((PROJ_TM, PROJ_TN), lambda j, i: (i, j)),
        out_shape=jax.ShapeDtypeStruct((NTOK, PROJ_W), F32),
        compiler_params=_cparams(("arbitrary", "arbitrary")),
        name="inproj",
    )(x_bf, wp)


def _pool_group(eg, lead, u_rows, pos, win, mw, mb, sc, take):
    s = eg
    sh = 1
    while sh < win:
        s = s + pltpu.roll(s, sh, axis=0)
        sh *= 2
    s = take(s)
    cnt = jnp.minimum(pos + 1, win).astype(F32)
    d = s / cnt - u_rows
    mixed = _dot(d.astype(BF), mw.astype(BF)) + mb
    return mixed * sc


def _pool_prompt_kernel(u_ref, mw_ref, mb_ref, sc_ref, o_ref, ext_ref):
    r = pl.program_id(1)
    halo = 2 * SUBLANES

    @pl.when(r == 0)
    def _():
        ext_ref[0:halo, :] = jnp.zeros((halo, POOL_DIM), F32)

    u = u_ref[...]
    ext_ref[halo:halo + POOL_TM, :] = u
    e = ext_ref[...]
    pos = r * POOL_TM + lax.broadcasted_iota(jnp.int32, (POOL_TM, 1), 0)
    for g, win in enumerate(POOL_WINDOWS):
        cs = slice(g * POOL_GROUP_DIM, (g + 1) * POOL_GROUP_DIM)
        out = _pool_group(e[:, cs], halo, u[:, cs], pos, win, mw_ref[g], mb_ref[:, cs], sc_ref[:, cs],
                          lambda s: s[halo:, :])
        o_ref[:, cs] = out.astype(BF)
    ext_ref[0:halo, :] = u[POOL_TM - halo:, :]


def _pool_prompt(proj, mw, mb, sc, layer):
    rt = SEQ // POOL_TM
    return pl.pallas_call(
        _pool_prompt_kernel,
        grid=(BATCH, rt),
        in_specs=[
            pl.BlockSpec((POOL_TM, POOL_DIM), lambda b, r: (b * rt + r, COL_U // POOL_DIM)),
            pl.BlockSpec((None, N_POOL_GROUPS, POOL_GROUP_DIM, POOL_GROUP_DIM), lambda b, r: (layer, 0, 0, 0)),
            pl.BlockSpec((None, 1, POOL_DIM), lambda b, r: (layer, 0, 0)),
            pl.BlockSpec((None, 1, POOL_DIM), lambda b, r: (layer, 0, 0)),
        ],
        out_specs=pl.BlockSpec((POOL_TM, POOL_DIM), lambda b, r: (b * rt + r, 0)),
        out_shape=jax.ShapeDtypeStruct((NTOK, POOL_DIM), BF),
        scratch_shapes=[pltpu.VMEM((POOL_TM + 2 * SUBLANES, POOL_DIM), F32)],
        compiler_params=_cparams(("arbitrary", "arbitrary")),
        name="pool_prompt",
    )(proj, mw, mb, sc)


def _pool_sample_kernel(ext_ref, mw_ref, mb_ref, sc_ref, a_in_ref, o_ref):
    del a_in_ref
    e = ext_ref[...]
    rows = POOL_SB * DEC_SEQ
    first = POOL_SLAB - DEC_SEQ
    pos = PAST_LEN + (lax.broadcasted_iota(jnp.int32, (rows, 1), 0) & (DEC_SEQ - 1))

    def take(s):
        s3 = s.reshape(POOL_SB, POOL_SLAB, POOL_GROUP_DIM)[:, first:, :]
        return s3.reshape(rows, POOL_GROUP_DIM)

    for g, win in enumerate(POOL_WINDOWS):
        cs = slice(g * POOL_GROUP_DIM, (g + 1) * POOL_GROUP_DIM)
        eg = e[:, cs]
        out = _pool_group(eg, first, take(eg), pos, win, mw_ref[g], mb_ref[:, cs], sc_ref[:, cs], take)
        o_ref[:, cs] = out.astype(BF)


def _pool_sample(ext_s, mw, mb, sc, a_out, layer):
    rows = POOL_SB * DEC_SEQ
    return pl.pallas_call(
        _pool_sample_kernel,
        grid=(DEC_BATCH // POOL_SB,),
        in_specs=[
            pl.BlockSpec((POOL_SB * POOL_SLAB, POOL_DIM), lambda i: (i, 0)),
            pl.BlockSpec((None, N_POOL_GROUPS, POOL_GROUP_DIM, POOL_GROUP_DIM), lambda i: (layer, 0, 0, 0)),
            pl.BlockSpec((None, 1, POOL_DIM), lambda i: (layer, 0, 0)),
            pl.BlockSpec((None, 1, POOL_DIM), lambda i: (layer, 0, 0)),
            pl.BlockSpec(memory_space=pl.ANY),
        ],
        out_specs=pl.BlockSpec((rows, POOL_DIM), lambda i: (NP // rows + i, 0)),
        out_shape=jax.ShapeDtypeStruct((NTOK, POOL_DIM), BF),
        input_output_aliases={4: 0},
        compiler_params=_cparams(("arbitrary",)),
        name="pool_sample",
    )(ext_s, mw, mb, sc, a_out)


def _ssd_prompt_kernel(xbc_ref, z_ref, dt_ref, cw_ref, cb_ref, dtb_ref, alog_ref, dsk_ref, nw_ref, e_ref,
                       yn_ref, hout_ref, ext_ref, h_ref):
    c = pl.program_id(1)
    q = CHUNK
    halo = SUBLANES

    @pl.when(c == 0)
    def _():
        ext_ref[0:halo, :] = jnp.zeros((halo, CONV_DIM), F32)
        h_ref[...] = jnp.zeros_like(h_ref)

    xbc = xbc_ref[...]
    ext_ref[halo:halo + q, :] = xbc
    first = halo - (CONV_W - 1)
    acc = ext_ref[first:first + q, :] * cw_ref[0:1, :]
    for k in range(1, CONV_W):
        acc = acc + ext_ref[first + k:first + k + q, :] * cw_ref[k:k + 1, :]
    ext_ref[0:halo, :] = xbc[q - halo:, :]
    conv = _silu(acc + cb_ref[...])

    dt = _softplus(dt_ref[...] + dtb_ref[...])
    a = -jnp.exp(alog_ref[...])
    da = dt * a
    ri = lax.broadcasted_iota(jnp.int32, (q, q), 0)
    ci = lax.broadcasted_iota(jnp.int32, (q, q), 1)
    causal = ri >= ci
    tril = jnp.where(causal, 1.0, 0.0).astype(BF)
    d1, d2, d3 = _split3(da)
    acum = _dot(tril, d1) + _dot(tril, d2) + _dot(tril, d3)
    acum_t = acum.T
    dt_t = dt.T
    a_last = acum[q - 1:q, :]
    e_bf = e_ref[...]
    wexp = _expand(jnp.exp(a_last - acum) * dt, e_bf)
    eexp = _expand(jnp.exp(acum), e_bf)
    lane = lax.broadcasted_iota(jnp.int32, (q, LANES), 1)
    low = lane < SSD_HEAD_DIM

    y_parts = []
    for g in range(SSD_GROUPS):
        gc = slice(g * GROUP_INNER, (g + 1) * GROUP_INNER)
        b_g = conv[:, SSD_INNER + g * D_STATE:SSD_INNER + (g + 1) * D_STATE].astype(BF)
        c0 = SSD_INNER + SSD_GROUPS * D_STATE
        c_g = conv[:, c0 + g * D_STATE:c0 + (g + 1) * D_STATE].astype(BF)
        cb = _dot_nt(c_g, b_g)
        xg = conv[:, gc]
        xg_bf = xg.astype(BF)
        yd = []
        for pr in range(HEADS_PER_GROUP // 2):
            ms = []
            for j in range(2):
                hh = g * HEADS_PER_GROUP + pr * 2 + j
                seg = acum[:, hh:hh + 1] - acum_t[hh:hh + 1, :]
                dec = jnp.exp(jnp.where(causal, seg, NEG))
                ms.append((cb * dec * dt_t[hh:hh + 1, :]).astype(BF))
            mp = jnp.concatenate(ms, axis=1)
            xp = xg_bf[:, pr * LANES:(pr + 1) * LANES]
            zero = jnp.zeros_like(xp)
            xbd = jnp.concatenate([jnp.where(low, xp, zero), jnp.where(low, zero, xp)], axis=0)
            yd.append(_dot(mp, xbd))
        yd = jnp.concatenate(yd, axis=1)
        hg = h_ref[g * GROUP_INNER:(g + 1) * GROUP_INNER, :]
        yoff = _dot_nt(c_g, hg.astype(BF)) * eexp[:, gc]
        y_parts.append(yd + yoff + xg * dsk_ref[:, gc])
        xw = (xg * wexp[:, gc]).astype(BF)
        st = _dot_tn(xw, b_g)
        for k in range(HEADS_PER_GROUP):
            hh = g * HEADS_PER_GROUP + k
            rows = slice(hh * SSD_HEAD_DIM, (hh + 1) * SSD_HEAD_DIM)
            cd = jnp.exp(a_last[:, hh:hh + 1])
            h_ref[rows, :] = h_ref[rows, :] * cd + st[k * SSD_HEAD_DIM:(k + 1) * SSD_HEAD_DIM, :]
    y = jnp.concatenate(y_parts, axis=1)
    yn_ref[...] = _gated_rmsnorm(y, z_ref[...], nw_ref[...]).astype(BF)

    @pl.when(c == pl.num_programs(1) - 1)
    def _():
        hout_ref[...] = h_ref[...]


def _ssd_prompt(proj, cw, cb, dtb, alog, dsk, nw, e_bf, layer):
    nc = SEQ // CHUNK
    vec = lambda w: pl.BlockSpec((None, 1, w), lambda b, c: (layer, 0, 0))
    return pl.pallas_call(
        _ssd_prompt_kernel,
        grid=(BATCH, nc),
        in_specs=[
            pl.BlockSpec((CHUNK, CONV_DIM), lambda b, c: (b * nc + c, COL_XBC // CONV_DIM)),
            pl.BlockSpec((CHUNK, SSD_INNER), lambda b, c: (b * nc + c, COL_Z // SSD_INNER)),
            pl.BlockSpec((CHUNK, LANES), lambda b, c: (b * nc + c, COL_DT // LANES)),
            pl.BlockSpec((None, CONV_W, CONV_DIM), lambda b, c: (layer, 0, 0)),
            vec(CONV_DIM), vec(LANES), vec(LANES), vec(SSD_INNER), vec(SSD_INNER),
            pl.BlockSpec((LANES, SSD_INNER), lambda b, c: (0, 0)),
        ],
        out_specs=[
            pl.BlockSpec((CHUNK, SSD_INNER), lambda b, c: (b * nc + c, 0)),
            pl.BlockSpec((None, SSD_INNER, D_STATE), lambda b, c: (b, 0, 0)),
        ],
        out_shape=[
            jax.ShapeDtypeStruct((NTOK, SSD_INNER), BF),
            jax.ShapeDtypeStruct((BATCH, SSD_INNER, D_STATE), F32),
        ],
        scratch_shapes=[
            pltpu.VMEM((CHUNK + SUBLANES, CONV_DIM), F32),
            pltpu.VMEM((SSD_INNER, D_STATE), F32),
        ],
        compiler_params=_cparams(("arbitrary", "arbitrary")),
        name="ssd_prompt",
    )(proj, proj, proj, cw, cb, dtb, alog, dsk, nw, e_bf)


def _ssd_sample_kernel(xbc_ref, z_ref, dt_ref, cs_ref, h0_ref, cw_ref, cb_ref, dtb_ref, alog_ref, dsk_ref,
                       nw_ref, e_ref, yn_in_ref, hs_in_ref, yn_ref, hout_ref):
    del yn_in_ref, hs_in_ref
    t = DEC_SEQ
    r = SSD_SB * t
    l_idx = lax.broadcasted_iota(jnp.int32, (r, 1), 0) & (t - 1)
    seq_idx = lax.broadcasted_iota(jnp.int32, (r, 1), 0) // t

    def bc(v, s):
        w = v.shape[1]
        v3 = v.reshape(SSD_SB, t, w)[:, s:s + 1, :]
        return jnp.broadcast_to(v3, (SSD_SB, t, w)).reshape(r, w)

    xbc = xbc_ref[...]
    st_rows = cs_ref[...]
    acc = None
    for k in range(CONV_W):
        m = CONV_W - 1 - k
        if m == 0:
            val = xbc
        else:
            cur = pltpu.roll(xbc, m, axis=0)
            back = CONV_W - 1 - m
            stv = st_rows if back == 0 else pltpu.roll(st_rows, r - back, axis=0)
            val = jnp.where(l_idx >= m, cur, stv)
        term = val * cw_ref[k:k + 1, :]
        acc = term if acc is None else acc + term
    conv = _silu(acc + cb_ref[...])

    dt = _softplus(dt_ref[...] + dtb_ref[...])
    a = -jnp.exp(alog_ref[...])
    acum = dt * a
    for sh in (1, 2, 4):
        acum = acum + jnp.where(l_idx >= sh, pltpu.roll(acum, sh, axis=0), 0.0)
    a_last = bc(acum, t - 1)
    e_bf = e_ref[...]
    wexp = _expand(jnp.exp(a_last - acum) * dt, e_bf)
    eexp = _expand(jnp.exp(acum), e_bf)
    cdl = jnp.exp(a_last)

    xs = conv[:, :SSD_INNER]
    bm = conv[:, SSD_INNER:SSD_INNER + SSD_GROUPS * D_STATE]
    cm = conv[:, SSD_INNER + SSD_GROUPS * D_STATE:]
    lane = lax.broadcasted_iota(jnp.int32, (r, LANES), 1)
    grp = lane // HEADS_PER_GROUP

    ms = []
    for s in range(t):
        prod = cm * bc(bm, s)
        cb = jnp.zeros((r, LANES), F32)
        for g in range(SSD_GROUPS):
            cbg = jnp.sum(prod[:, g * D_STATE:(g + 1) * D_STATE], axis=-1, keepdims=True)
            cb = jnp.where(grp == g, cbg, cb)
        dec = jnp.exp(jnp.where(l_idx >= s, acum - bc(acum, s), NEG))
        ms.append(cb * dec * bc(dt, s))
    mexp = _expand(jnp.concatenate(ms, axis=0), e_bf)
    yd = mexp[0:r, :] * bc(xs, 0)
    for s in range(1, t):
        yd = yd + mexp[s * r:(s + 1) * r, :] * bc(xs, s)

    xw = xs * wexp
    c_bf = cm.astype(BF)
    b_bf = bm.astype(BF)
    yoff_parts = []
    for g in range(SSD_GROUPS):
        gc = slice(g * GROUP_INNER, (g + 1) * GROUP_INNER)
        sc = slice(g * D_STATE, (g + 1) * D_STATE)
        hcat = h0_ref[:, gc, :].reshape(SSD_SB * GROUP_INNER, D_STATE)
        full = _dot_nt(c_bf[:, sc], hcat.astype(BF))
        yo = jnp.zeros((r, GROUP_INNER), F32)
        for qi in range(SSD_SB):
            yo = jnp.where(seq_idx == qi, full[:, qi * GROUP_INNER:(qi + 1) * GROUP_INNER], yo)
        yoff_parts.append(yo)
        for qi in range(SSD_SB):
            xq = jnp.where(seq_idx == qi, xw[:, gc], 0.0).astype(BF)
            st = _dot_tn(xq, b_bf[:, sc])
            for k in range(HEADS_PER_GROUP):
                hh = g * HEADS_PER_GROUP + k
                rows = slice(hh * SSD_HEAD_DIM, (hh + 1) * SSD_HEAD_DIM)
                cd = cdl[qi * t:qi * t + 1, hh:hh + 1]
                hout_ref[qi, rows, :] = h0_ref[qi, rows, :] * cd + st[k * SSD_HEAD_DIM:(k + 1) * SSD_HEAD_DIM, :]
    yoff = jnp.concatenate(yoff_parts, axis=1) * eexp
    y = yd + yoff + xs * dsk_ref[...]
    yn_ref[...] = _gated_rmsnorm(y, z_ref[...], nw_ref[...]).astype(BF)


def _ssd_sample(proj, cs_pad, h0_all, cw, cb, dtb, alog, dsk, nw, e_bf, yn, hs_prev, layer):
    r = SSD_SB * DEC_SEQ
    base = NP // r
    vec = lambda w: pl.BlockSpec((None, 1, w), lambda i: (layer, 0, 0))
    args = [proj, proj, proj, cs_pad, h0_all, cw, cb, dtb, alog, dsk, nw, e_bf, yn]
    in_specs = [
        pl.BlockSpec((r, CONV_DIM), lambda i: (base + i, COL_XBC // CONV_DIM)),
        pl.BlockSpec((r, SSD_INNER), lambda i: (base + i, COL_Z // SSD_INNER)),
        pl.BlockSpec((r, LANES), lambda i: (base + i, COL_DT // LANES)),
        pl.BlockSpec((None, r, CONV_DIM), lambda i: (layer, i, 0)),
        pl.BlockSpec((None, SSD_SB, SSD_INNER, D_STATE), lambda i: (layer, i, 0, 0)),
        pl.BlockSpec((None, CONV_W, CONV_DIM), lambda i: (layer, 0, 0)),
        vec(CONV_DIM), vec(LANES), vec(LANES), vec(SSD_INNER), vec(SSD_INNER),
        pl.BlockSpec((LANES, SSD_INNER), lambda i: (0, 0)),
        pl.BlockSpec(memory_space=pl.ANY),
    ]
    aliases = {12: 0}
    if hs_prev is not None:
        args.append(hs_prev)
        in_specs.append(pl.BlockSpec(memory_space=pl.ANY))
        aliases[13] = 1
        body = _ssd_sample_kernel
    else:
        body = functools.partial(_ssd_sample_kernel_first)
    return pl.pallas_call(
        body,
        grid=(DEC_BATCH // SSD_SB,),
        in_specs=in_specs,
        out_specs=[
            pl.BlockSpec((r, SSD_INNER), lambda i: (base + i, 0)),
            pl.BlockSpec((None, SSD_SB, SSD_INNER, D_STATE), lambda i: (layer, i, 0, 0)),
        ],
        out_shape=[
            jax.ShapeDtypeStruct((NTOK, SSD_INNER), BF),
            jax.ShapeDtypeStruct((DEPTH, DEC_BATCH, SSD_INNER, D_STATE), F32),
        ],
        input_output_aliases=aliases,
        compiler_params=_cparams(("arbitrary",)),
        name="ssd_sample",
    )(*args)


def _ssd_sample_kernel_first(*refs):
    ins, outs = refs[:13], refs[13:]
    _ssd_sample_kernel(*ins, None, *outs)


def _mix_kernel(a_ref, yn_ref, ga_ref, gb_ref, x_ref, wpo_ref, wso_ref, wo_ref, gab_ref, gbb_ref,
                g1_ref, b1_ref, wr_ref, rb_ref, x1_ref, rt_ref):
    pa = _dot(a_ref[...], wpo_ref[...])
    ps = _dot(yn_ref[...], wso_ref[...])
    m = jax.nn.sigmoid(ga_ref[...] + gab_ref[...]) * pa + jax.nn.sigmoid(gb_ref[...] + gbb_ref[...]) * ps
    res = DN_ALPHA * x_ref[...] + _dot(m.astype(BF), wo_ref[...])
    x1 = _layer_norm(res, g1_ref[...], b1_ref[...])
    x1_ref[...] = x1

    xh, xl = _split2(x1)
    wh, wl = _split2(wr_ref[...])
    logits = _dot(xh, wh) + _dot(xh, wl) + _dot(xl, wh) + rb_ref[...]
    tm = logits.shape[0]
    lane = lax.broadcasted_iota(jnp.int32, (tm, LANES), 1)
    big = 4 * LANES
    is_g = (lane >= N_EXPERTS) & (lane < N_EXPERTS + N_EXPERT_GROUPS)
    gl = jnp.where(is_g, logits, NEG)
    gmax = jnp.max(gl, axis=-1, keepdims=True)
    gidx = jnp.min(jnp.where(gl == gmax, lane, big), axis=-1, keepdims=True) - N_EXPERTS
    gsum = jnp.sum(jnp.where(is_g, jnp.exp(gl - gmax), 0.0), axis=-1, keepdims=True)
    gval = 1.0 / gsum
    in_grp = (lane < N_EXPERTS) & ((lane // EXPERTS_PER_GROUP) == gidx)
    el = jnp.where(in_grp, logits, NEG)
    m1 = jnp.max(el, axis=-1, keepdims=True)
    i1 = jnp.min(jnp.where(el == m1, lane, big), axis=-1, keepdims=True)
    el2 = jnp.where(lane == i1, NEG, el)
    m2 = jnp.max(el2, axis=-1, keepdims=True)
    i2 = jnp.min(jnp.where(el2 == m2, lane, big), axis=-1, keepdims=True)
    r21 = jnp.exp(m2 - m1)
    w1 = gval / (1.0 + r21)
    w2 = w1 * r21
    rt = jnp.where(lane == 0, i1.astype(F32),
                   jnp.where(lane == 1, i2.astype(F32),
                             jnp.where(lane == 2, w1, jnp.where(lane == 3, w2, 0.0))))
    rt_ref[...] = rt


def _mix(a_out, yn, proj, x, wpo, wso, wo, gab, gbb, g1, b1, wr, rb, layer):
    const = lambda shape: pl.BlockSpec((None,) + shape, lambda i: (layer,) + (0,) * len(shape),
                                       pipeline_mode=pl.Buffered(1))
    vec = lambda w: pl.BlockSpec((None, 1, w), lambda i: (layer, 0, 0))
    return pl.pallas_call(
        _mix_kernel,
        grid=(NTOK // MIX_TM,),
        in_specs=[
            pl.BlockSpec((MIX_TM, POOL_DIM), lambda i: (i, 0)),
            pl.BlockSpec((MIX_TM, SSD_INNER), lambda i: (i, 0)),
            pl.BlockSpec((MIX_TM, D_MODEL), lambda i: (i, COL_GA // D_MODEL)),
            pl.BlockSpec((MIX_TM, D_MODEL), lambda i: (i, COL_GB // D_MODEL)),
            pl.BlockSpec((MIX_TM, D_MODEL), lambda i: (i, 0)),
            const((POOL_DIM, D_MODEL)), const((SSD_INNER, D_MODEL)), const((D_MODEL, D_MODEL)),
            vec(D_MODEL), vec(D_MODEL), vec(D_MODEL), vec(D_MODEL),
            const((D_MODEL, LANES)), vec(LANES),
        ],
        out_specs=[
            pl.BlockSpec((MIX_TM, D_MODEL), lambda i: (i, 0)),
            pl.BlockSpec((MIX_TM, LANES), lambda i: (i, 0)),
        ],
        out_shape=[
            jax.ShapeDtypeStruct((NTOK, D_MODEL), F32),
            jax.ShapeDtypeStruct((NTOK, LANES), F32),
        ],
        compiler_params=_cparams(("arbitrary",)),
        name="mix",
    )(a_out, yn, proj, proj, x, wpo, wso, wo, gab, gbb, g1, b1, wr, rb)


def _route_tables(rt):
    e = jnp.concatenate([rt[:, 0], rt[:, 1]]).astype(jnp.int32)
    oh = (e[:, None] == jnp.arange(N_EXPERTS, dtype=jnp.int32)[None, :]).astype(jnp.int32)
    cs = jnp.cumsum(oh, axis=0)
    rank = jnp.sum((cs - oh) * oh, axis=1)
    counts = cs[-1]
    padded = ((counts + MOE_T - 1) // MOE_T) * MOE_T
    ends = jnp.cumsum(padded)
    off = ends - padded
    pos = (jnp.take(off, e) + rank).astype(jnp.int32)
    n_used = (ends[-1] // MOE_T).astype(jnp.int32)
    tiles = jnp.arange(MOE_TILES, dtype=jnp.int32)
    tile_blk = jnp.minimum(tiles, n_used - 1)
    tile_e = jnp.minimum(jnp.searchsorted(ends // MOE_T, tile_blk, side="right"), N_EXPERTS - 1).astype(jnp.int32)
    pad_lo = (ends - MOE_T).astype(jnp.int32)
    return pos, tile_blk, tile_e, n_used.reshape(1), pad_lo, counts.astype(jnp.int32)


def _row_copy(src, dst, si, di, sem):
    return pltpu.make_async_copy(src.at[pl.ds(si, 1)], dst.at[pl.ds(di, 1)], sem)


def _dispatch_kernel(pos_ref, padlo_ref, cnt_ref, x_hbm, xs_hbm, zero_ref, sem):
    i = pl.program_id(0)

    @pl.when(i == 0)
    def _():
        zero_ref[...] = jnp.zeros_like(zero_ref)

        def zero_copy(e):
            lo = pl.multiple_of(padlo_ref[e], MOE_T)
            return pltpu.make_async_copy(zero_ref, xs_hbm.at[pl.ds(lo, MOE_T)], sem)

        def zstart(e, c):
            @pl.when(cnt_ref[e] > 0)
            def _():
                zero_copy(e).start()
            return c

        def zwait(e, c):
            @pl.when(cnt_ref[e] > 0)
            def _():
                zero_copy(e).wait()
            return c

        lax.fori_loop(0, N_EXPERTS, zstart, 0)
        lax.fori_loop(0, N_EXPERTS, zwait, 0)

    t0 = i * DISP_TD

    def start(j, c):
        t = t0 + j
        _row_copy(x_hbm, xs_hbm, t, pos_ref[t], sem).start()
        _row_copy(x_hbm, xs_hbm, t, pos_ref[NTOK + t], sem).start()
        return c

    def wait(j, c):
        t = t0 + j
        _row_copy(x_hbm, xs_hbm, t, pos_ref[t], sem).wait()
        _row_copy(x_hbm, xs_hbm, t, pos_ref[NTOK + t], sem).wait()
        return c

    lax.fori_loop(0, DISP_TD, start, 0)
    lax.fori_loop(0, DISP_TD, wait, 0)


def _dispatch(pos, pad_lo, counts, x1):
    return pl.pallas_call(
        _dispatch_kernel,
        grid_spec=pltpu.PrefetchScalarGridSpec(
            num_scalar_prefetch=3,
            grid=(NTOK // DISP_TD,),
            in_specs=[pl.BlockSpec(memory_space=pl.ANY)],
            out_specs=pl.BlockSpec(memory_space=pl.ANY),
            scratch_shapes=[pltpu.VMEM((MOE_T, D_MODEL), F32), pltpu.SemaphoreType.DMA(())],
        ),
        out_shape=jax.ShapeDtypeStruct((MOE_ROWS, D_MODEL), F32),
        compiler_params=pltpu.CompilerParams(dimension_semantics=("arbitrary",), has_side_effects=True),
        name="moe_dispatch",
    )(pos, pad_lo, counts, x1)


def _expert_kernel(blk_ref, te_ref, nu_ref, x_ref, wg_ref, wu_ref, wd_ref, o_ref, wg_bf, wu_bf, wd_bf):
    i = pl.program_id(0)
    prev = te_ref[jnp.maximum(i - 1, 0)]

    @pl.when((i == 0) | (te_ref[i] != prev))
    def _():
        wg_bf[...] = wg_ref[...].astype(BF)
        wu_bf[...] = wu_ref[...].astype(BF)
        wd_bf[...] = wd_ref[...].astype(BF)

    @pl.when(i < nu_ref[0])
    def _():
        x = x_ref[...].astype(BF)
        h = _silu(_dot(x, wg_bf[...])) * _dot(x, wu_bf[...])
        o_ref[...] = _dot(h.astype(BF), wd_bf[...])


def _experts(tile_blk, tile_e, n_used, xs, wg, wu, wd, layer):
    wspec = lambda a, b: pl.BlockSpec((None, None, a, b), lambda i, blk, te, nu: (layer, te[i], 0, 0))
    return pl.pallas_call(
        _expert_kernel,
        grid_spec=pltpu.PrefetchScalarGridSpec(
            num_scalar_prefetch=3,
            grid=(MOE_TILES,),
            in_specs=[
                pl.BlockSpec((MOE_T, D_MODEL), lambda i, blk, te, nu: (blk[i], 0)),
                wspec(D_MODEL, D_EXPERT), wspec(D_MODEL, D_EXPERT), wspec(D_EXPERT, D_MODEL),
            ],
            out_specs=pl.BlockSpec((MOE_T, D_MODEL), lambda i, blk, te, nu: (blk[i], 0)),
            scratch_shapes=[
                pltpu.VMEM((D_MODEL, D_EXPERT), BF),
                pltpu.VMEM((D_MODEL, D_EXPERT), BF),
                pltpu.VMEM((D_EXPERT, D_MODEL), BF),
            ],
        ),
        out_shape=jax.ShapeDtypeStruct((MOE_ROWS, D_MODEL), F32),
        compiler_params=_cparams(("arbitrary",)),
        name="moe_experts",
    )(tile_blk, tile_e, n_used, xs, wg, wu, wd)


def _combine_kernel(pos_ref, ys_hbm, x1_ref, rt_ref, g2_ref, b2_ref, x2_ref, xbf_ref, buf_ref, sem):
    i = pl.program_id(0)
    t0 = i * COMB_TC

    def start(j, c):
        t = t0 + j
        _row_copy(ys_hbm, buf_ref.at[0], pos_ref[t], j, sem).start()
        _row_copy(ys_hbm, buf_ref.at[1], pos_ref[NTOK + t], j, sem).start()
        return c

    def wait(j, c):
        t = t0 + j
        _row_copy(ys_hbm, buf_ref.at[0], pos_ref[t], j, sem).wait()
        _row_copy(ys_hbm, buf_ref.at[1], pos_ref[NTOK + t], j, sem).wait()
        return c

    lax.fori_loop(0, COMB_TC, start, 0)
    lax.fori_loop(0, COMB_TC, wait, 0)
    rt = rt_ref[...]
    f = rt[:, 2:3] * buf_ref[0] + rt[:, 3:4] * buf_ref[1]
    x2 = _layer_norm(DN_ALPHA * x1_ref[...] + f, g2_ref[...], b2_ref[...])
    x2_ref[...] = x2
    xbf_ref[...] = x2.astype(BF)


def _combine(pos, ys, x1, rt, g2, b2, layer):
    vec = lambda w: pl.BlockSpec((None, 1, w), lambda i, pos: (layer, 0, 0))
    return pl.pallas_call(
        _combine_kernel,
        grid_spec=pltpu.PrefetchScalarGridSpec(
            num_scalar_prefetch=1,
            grid=(NTOK // COMB_TC,),
            in_specs=[
                pl.BlockSpec(memory_space=pl.ANY),
                pl.BlockSpec((COMB_TC, D_MODEL), lambda i, pos: (i, 0)),
                pl.BlockSpec((COMB_TC, LANES), lambda i, pos: (i, 0)),
                vec(D_MODEL), vec(D_MODEL),
            ],
            out_specs=[
                pl.BlockSpec((COMB_TC, D_MODEL), lambda i, pos: (i, 0)),
                pl.BlockSpec((COMB_TC, D_MODEL), lambda i, pos: (i, 0)),
            ],
            scratch_shapes=[pltpu.VMEM((2, COMB_TC, D_MODEL), F32), pltpu.SemaphoreType.DMA(())],
        ),
        out_shape=[
            jax.ShapeDtypeStruct((NTOK, D_MODEL), F32),
            jax.ShapeDtypeStruct((NTOK, D_MODEL), BF),
        ],
        compiler_params=_cparams(("arbitrary",)),
        name="moe_combine",
    )(pos, ys, x1, rt, g2, b2)


def kernel(x_prompt, x_sample, state_ssm, state_conv, state_pool, w_in, pool_map_w, pool_map_b, pool_scale,
           conv_w, conv_b, dt_bias, a_log, d_skip, ssd_norm_w, gate_a_bias, gate_b_bias, w_pool_out, w_ssd_out,
           w_o, ln1_g, ln1_b, router_group_w, router_group_b, router_expert_w, router_expert_b, expert_w_gate,
           expert_w_up, expert_w_down, ln2_g, ln2_b):
    u_w, z_w, xbc_w, dt_w, ga_w, gb_w = jnp.split(w_in, SPLIT_POINTS, axis=-1)
    dt_pad = jnp.zeros((DEPTH, D_MODEL, LANES - SSD_HEADS), F32)
    wp = jnp.concatenate([z_w, ga_w, gb_w, xbc_w, u_w, dt_w, dt_pad], axis=-1).astype(BF)
    wpo = w_pool_out.astype(BF)
    wso = w_ssd_out.astype(BF)
    wo = w_o.astype(BF)
    r_pad = jnp.zeros((DEPTH, D_MODEL, LANES - N_EXPERTS - N_EXPERT_GROUPS), F32)
    wr = jnp.concatenate([router_expert_w, router_group_w, r_pad], axis=-1)
    rb = jnp.concatenate([router_expert_b, router_group_b,
                          jnp.zeros((DEPTH, LANES - N_EXPERTS - N_EXPERT_GROUPS), F32)], axis=-1)[:, None, :]
    row = lambda v: v[:, None, :]
    head_pad = lambda v: jnp.pad(v, ((0, 0), (0, LANES - SSD_HEADS)))[:, None, :]
    dtb = head_pad(dt_bias)
    alog = head_pad(a_log)
    dsk = row(jnp.repeat(d_skip, SSD_HEAD_DIM, axis=-1))
    e_bf = (jnp.arange(SSD_INNER)[None, :] // SSD_HEAD_DIM == jnp.arange(LANES)[:, None]).astype(BF)
    wg = expert_w_gate.reshape(DEPTH, N_EXPERTS, D_MODEL, D_EXPERT)
    wu = expert_w_up.reshape(DEPTH, N_EXPERTS, D_MODEL, D_EXPERT)
    wd = expert_w_down.reshape(DEPTH, N_EXPERTS, D_EXPERT, D_MODEL)
    cs_pad = jnp.pad(state_conv, ((0, 0), (0, 0), (0, DEC_SEQ - (CONV_W - 1)), (0, 0)))
    cs_pad = cs_pad.reshape(DEPTH, NS, CONV_DIM)
    h0_all = state_ssm.reshape(DEPTH, DEC_BATCH, SSD_INNER, D_STATE)

    x = jnp.concatenate([x_prompt.reshape(NP, D_MODEL), x_sample.reshape(NS, D_MODEL)], axis=0)
    x_bf = x.astype(BF)

    ssm_p, conv_p, pool_p, conv_s, pool_s = [], [], [], [], []
    hs_all = None
    for l in range(DEPTH):
        proj = _inproj(x_bf, wp, l)
        u_s = proj[NP:, COL_U:COL_U + POOL_DIM].reshape(DEC_BATCH, DEC_SEQ, POOL_DIM)
        xbc_s = proj[NP:, COL_XBC:COL_XBC + CONV_DIM].reshape(DEC_BATCH, DEC_SEQ, CONV_DIM)
        pool_p.append(proj[:NP, COL_U:COL_U + POOL_DIM].reshape(BATCH, SEQ, POOL_DIM)[:, SEQ - POOL_BUF:])
        conv_p.append(proj[:NP, COL_XBC:COL_XBC + CONV_DIM].reshape(BATCH, SEQ, CONV_DIM)[:, SEQ - (CONV_W - 1):])
        pool_s.append(jnp.concatenate([state_pool[l][:, DEC_SEQ:], u_s], axis=1))
        conv_s.append(xbc_s[:, DEC_SEQ - (CONV_W - 1):])

        a_out = _pool_prompt(proj, pool_map_w, row(pool_map_b.reshape(DEPTH, POOL_DIM)), row(pool_scale), l)
        ext_s = jnp.concatenate([jnp.zeros((DEC_BATCH, 1, POOL_DIM), F32), state_pool[l], u_s], axis=1)
        a_out = _pool_sample(ext_s.reshape(DEC_BATCH * POOL_SLAB, POOL_DIM), pool_map_w,
                             row(pool_map_b.reshape(DEPTH, POOL_DIM)), row(pool_scale), a_out, l)
        yn, h_p = _ssd_prompt(proj, conv_w, row(conv_b), dtb, alog, dsk, row(ssd_norm_w), e_bf, l)
        ssm_p.append(h_p)
        yn, hs_all = _ssd_sample(proj, cs_pad, h0_all, conv_w, row(conv_b), dtb, alog, dsk, row(ssd_norm_w),
                                 e_bf, yn, hs_all, l)
        x1, rt = _mix(a_out, yn, proj, x, wpo, wso, wo, row(gate_a_bias), row(gate_b_bias), row(ln1_g),
                      row(ln1_b), wr, rb, l)
        pos, tile_blk, tile_e, n_used, pad_lo, counts = _route_tables(rt)
        xs = _dispatch(pos, pad_lo, counts, x1)
        ys = _experts(tile_blk, tile_e, n_used, xs, wg, wu, wd, l)
        x, x_bf = _combine(pos, ys, x1, rt, row(ln2_g), row(ln2_b), l)

    y_prompt = x[:NP].reshape(BATCH, SEQ, D_MODEL)
    y_sample = x[NP:].reshape(DEC_BATCH, DEC_SEQ, D_MODEL)
    new_ssm_prompt = jnp.stack(ssm_p, 0).reshape(DEPTH, BATCH, SSD_HEADS, SSD_HEAD_DIM, D_STATE)
    new_ssm_sample = hs_all.reshape(DEPTH, DEC_BATCH, SSD_HEADS, SSD_HEAD_DIM, D_STATE)
    return (y_prompt, y_sample, new_ssm_prompt, jnp.stack(conv_p, 0), jnp.stack(pool_p, 0),
            new_ssm_sample, jnp.stack(conv_s, 0), jnp.stack(pool_s, 0))
```

```python
import jax
import jax.numpy as jnp
from jax import lax
from jax.experimental import pallas as pl
from jax.experimental.pallas import tpu as pltpu

F32 = jnp.float32
BF = jnp.bfloat16

D_MODEL = 2048
BATCH = 4
SEQ = 2048
DEPTH = 4
DEC_BATCH = 128
DEC_SEQ = 8
PAST_LEN = 16384
POOL_WINDOWS = (2, 4, 8, 16)
N_POOL_GROUPS = 4
POOL_DIM = D_MODEL // 2
POOL_GROUP_DIM = POOL_DIM // N_POOL_GROUPS
POOL_BUF = max(POOL_WINDOWS) - 1
SSD_INNER = D_MODEL
SSD_HEAD_DIM = 64
SSD_HEADS = SSD_INNER // SSD_HEAD_DIM
SSD_GROUPS = 4
HEADS_PER_GROUP = SSD_HEADS // SSD_GROUPS
GROUP_INNER = SSD_INNER // SSD_GROUPS
D_STATE = 128
CONV_W = 4
CONV_DIM = SSD_INNER + 2 * SSD_GROUPS * D_STATE
CHUNK = 128
N_EXPERT_GROUPS = 4
EXPERTS_PER_GROUP = 8
N_EXPERTS = N_EXPERT_GROUPS * EXPERTS_PER_GROUP
D_EXPERT = D_MODEL // 4
DN_ALPHA = (2.0 * DEPTH) ** 0.25
LN_EPS = 1e-5
RMS_EPS = 1e-5
SPLIT_POINTS = (1024, 3072, 6144, 6176, 8224)

NP = BATCH * SEQ
NS = DEC_BATCH * DEC_SEQ
NTOK = NP + NS

LANES = 128
SUBLANES = 8
VMEM_LIMIT = 56 * 1024 * 1024

COL_U = 0
COL_Z = POOL_DIM
COL_XBC = POOL_DIM + SSD_INNER
MAIN_W = POOL_DIM + SSD_INNER + CONV_DIM
COL_GA = 0
COL_GB = D_MODEL
COL_DT = 2 * D_MODEL
GATE_W = COL_DT + LANES

NEG = -1e30

PROJ_TM = 1024
MAIN_TN = 1024
GATE_TN = GATE_W // 3
POOL_TM = 512
POOL_SB = 16
POOL_SLAB = 24
SSD_SB = 4
MIX_TM = 256
MOE_T = 256
MOE_TILES = 2 * NTOK // MOE_T + N_EXPERTS
MOE_ROWS = MOE_TILES * MOE_T
DISP_TD = 512
COMB_TC = 256


def _cparams(sem, vmem=VMEM_LIMIT):
    return pltpu.CompilerParams(dimension_semantics=sem, vmem_limit_bytes=vmem)


def _dot(a, b):
    return jnp.dot(a, b, preferred_element_type=F32)


def _dot_nt(a, b):
    return lax.dot_general(a, b, (((1,), (1,)), ((), ())), preferred_element_type=F32)


def _dot_tn(a, b):
    return lax.dot_general(a, b, (((0,), (0,)), ((), ())), preferred_element_type=F32)


def _split2(v):
    hi = v.astype(BF)
    lo = (v - hi.astype(F32)).astype(BF)
    return hi, lo


def _split3(v):
    h1 = v.astype(BF)
    r1 = v - h1.astype(F32)
    h2 = r1.astype(BF)
    h3 = (r1 - h2.astype(F32)).astype(BF)
    return h1, h2, h3


def _expand(v, e_bf):
    hi, lo = _split2(v)
    return _dot(hi, e_bf) + _dot(lo, e_bf)


def _silu(x):
    return x * jax.nn.sigmoid(x)


def _softplus(x):
    return jnp.maximum(x, 0.0) + jnp.log1p(jnp.exp(-jnp.abs(x)))


def _layer_norm(x, g, b):
    mu = jnp.mean(x, axis=-1, keepdims=True)
    xc = x - mu
    var = jnp.mean(xc * xc, axis=-1, keepdims=True)
    return xc * lax.rsqrt(var + LN_EPS) * g + b


def _gated_rmsnorm(y, z, nw):
    v = y * _silu(z)
    outs = []
    for g in range(SSD_GROUPS):
        vg = v[:, g * GROUP_INNER:(g + 1) * GROUP_INNER]
        ms = jnp.mean(vg * vg, axis=-1, keepdims=True)
        outs.append(vg * lax.rsqrt(ms + RMS_EPS))
    return jnp.concatenate(outs, axis=-1) * nw


def _inproj_main_kernel(x_ref, w_ref, o_ref, wbf_ref):
    @pl.when(pl.program_id(1) == 0)
    def _():
        wbf_ref[...] = w_ref[...].astype(BF)

    o_ref[...] = _dot(x_ref[...], wbf_ref[...])


def _inproj_main(x_bf, w_in, layer):
    return pl.pallas_call(
        _inproj_main_kernel,
        grid=(MAIN_W // MAIN_TN, NTOK // PROJ_TM),
        in_specs=[
            pl.BlockSpec((PROJ_TM, D_MODEL), lambda j, i: (i, 0)),
            pl.BlockSpec((None, D_MODEL, MAIN_TN), lambda j, i: (layer, 0, j)),
        ],
        out_specs=pl.BlockSpec((PROJ_TM, MAIN_TN), lambda j, i: (i, j)),
        out_shape=jax.ShapeDtypeStruct((NTOK, MAIN_W), F32),
        scratch_shapes=[pltpu.VMEM((D_MODEL, MAIN_TN), BF)],
        compiler_params=_cparams(("arbitrary", "arbitrary")),
        name="inproj_main",
    )(x_bf, w_in)


def _inproj_gate_kernel(x_ref, w_ref, o_ref):
    o_ref[...] = _dot(x_ref[...], w_ref[...])


def _inproj_gate(x_bf, wgate, layer):
    return pl.pallas_call(
        _inproj_gate_kernel,
        grid=(GATE_W // GATE_TN, NTOK // PROJ_TM),
        in_specs=[
            pl.BlockSpec((PROJ_TM, D_MODEL), lambda j, i: (i, 0)),
            pl.BlockSpec((None, D_MODEL, GATE_TN), lambda j, i: (layer, 0, j)),
        ],
        out_specs=pl.BlockSpec((PROJ_TM, GATE_TN), lambda j, i: (i, j)),
        out_shape=jax.ShapeDtypeStruct((NTOK, GATE_W), F32),
        compiler_params=_cparams(("arbitrary", "arbitrary")),
        name="inproj_gate",
    )(x_bf, wgate)


def _pool_group(eg, lead, u_rows, pos, win, mw, mb, sc, take):
    s = eg
    sh = 1
    while sh < win:
        s = s + pltpu.roll(s, sh, axis=0)
        sh *= 2
    s = take(s)
    cnt = jnp.minimum(pos + 1, win).astype(F32)
    d = s / cnt - u_rows
    mixed = _dot(d.astype(BF), mw.astype(BF)) + mb
    return mixed * sc


def _pool_prompt_kernel(u_ref, mw_ref, mb_ref, sc_ref, o_ref, ext_ref):
    r = pl.program_id(1)
    halo = 2 * SUBLANES

    @pl.when(r == 0)
    def _():
        ext_ref[0:halo, :] = jnp.zeros((halo, POOL_DIM), F32)

    u = u_ref[...]
    ext_ref[halo:halo + POOL_TM, :] = u
    e = ext_ref[...]
    pos = r * POOL_TM + lax.broadcasted_iota(jnp.int32, (POOL_TM, 1), 0)
    for g, win in enumerate(POOL_WINDOWS):
        cs = slice(g * POOL_GROUP_DIM, (g + 1) * POOL_GROUP_DIM)
        out = _pool_group(e[:, cs], halo, u[:, cs], pos, win, mw_ref[g], mb_ref[:, cs], sc_ref[:, cs],
                          lambda s: s[halo:, :])
        o_ref[:, cs] = out.astype(BF)
    ext_ref[0:halo, :] = u[POOL_TM - halo:, :]


def _pool_prompt(proj, mw, mb, sc, layer):
    rt = SEQ // POOL_TM
    return pl.pallas_call(
        _pool_prompt_kernel,
        grid=(BATCH, rt),
        in_specs=[
            pl.BlockSpec((POOL_TM, POOL_DIM), lambda b, r: (b * rt + r, COL_U // POOL_DIM)),
            pl.BlockSpec((None, N_POOL_GROUPS, POOL_GROUP_DIM, POOL_GROUP_DIM), lambda b, r: (layer, 0, 0, 0)),
            pl.BlockSpec((None, 1, POOL_DIM), lambda b, r: (layer, 0, 0)),
            pl.BlockSpec((None, 1, POOL_DIM), lambda b, r: (layer, 0, 0)),
        ],
        out_specs=pl.BlockSpec((POOL_TM, POOL_DIM), lambda b, r: (b * rt + r, 0)),
        out_shape=jax.ShapeDtypeStruct((NP, POOL_DIM), BF),
        scratch_shapes=[pltpu.VMEM((POOL_TM + 2 * SUBLANES, POOL_DIM), F32)],
        compiler_params=_cparams(("arbitrary", "arbitrary")),
        name="pool_prompt",
    )(proj, mw, mb, sc)


def _pool_sample_kernel(ext_ref, mw_ref, mb_ref, sc_ref, o_ref):
    e = ext_ref[...]
    rows = POOL_SB * DEC_SEQ
    first = POOL_SLAB - DEC_SEQ
    pos = PAST_LEN + (lax.broadcasted_iota(jnp.int32, (rows, 1), 0) & (DEC_SEQ - 1))

    def take(s):
        s3 = s.reshape(POOL_SB, POOL_SLAB, POOL_GROUP_DIM)[:, first:, :]
        return s3.reshape(rows, POOL_GROUP_DIM)

    for g, win in enumerate(POOL_WINDOWS):
        cs = slice(g * POOL_GROUP_DIM, (g + 1) * POOL_GROUP_DIM)
        eg = e[:, cs]
        out = _pool_group(eg, first, take(eg), pos, win, mw_ref[g], mb_ref[:, cs], sc_ref[:, cs], take)
        o_ref[:, cs] = out.astype(BF)


def _pool_sample(ext_s, mw, mb, sc, layer):
    rows = POOL_SB * DEC_SEQ
    return pl.pallas_call(
        _pool_sample_kernel,
        grid=(DEC_BATCH // POOL_SB,),
        in_specs=[
            pl.BlockSpec((POOL_SB * POOL_SLAB, POOL_DIM), lambda i: (i, 0)),
            pl.BlockSpec((None, N_POOL_GROUPS, POOL_GROUP_DIM, POOL_GROUP_DIM), lambda i: (layer, 0, 0, 0)),
            pl.BlockSpec((None, 1, POOL_DIM), lambda i: (layer, 0, 0)),
            pl.BlockSpec((None, 1, POOL_DIM), lambda i: (layer, 0, 0)),
        ],
        out_specs=pl.BlockSpec((rows, POOL_DIM), lambda i: (i, 0)),
        out_shape=jax.ShapeDtypeStruct((NS, POOL_DIM), BF),
        compiler_params=_cparams(("arbitrary",)),
        name="pool_sample",
    )(ext_s, mw, mb, sc)


def _ssd_prompt_kernel(xbc_ref, z0_ref, z1_ref, dt_ref, cw_ref, cb_ref, dtb_ref, alog_ref, dsk_ref, nw_ref, e_ref,
                       yn_ref, hout_ref, ext_ref, h_ref):
    c = pl.program_id(1)
    q = CHUNK
    halo = SUBLANES

    @pl.when(c == 0)
    def _():
        ext_ref[0:halo, :] = jnp.zeros((halo, CONV_DIM), F32)
        h_ref[...] = jnp.zeros_like(h_ref)

    xbc = xbc_ref[...]
    ext_ref[halo:halo + q, :] = xbc
    first = halo - (CONV_W - 1)
    acc = ext_ref[first:first + q, :] * cw_ref[0:1, :]
    for k in range(1, CONV_W):
        acc = acc + ext_ref[first + k:first + k + q, :] * cw_ref[k:k + 1, :]
    ext_ref[0:halo, :] = xbc[q - halo:, :]
    conv = _silu(acc + cb_ref[...])

    dt = _softplus(dt_ref[...] + dtb_ref[...])
    a = -jnp.exp(alog_ref[...])
    da = dt * a
    ri = lax.broadcasted_iota(jnp.int32, (q, q), 0)
    ci = lax.broadcasted_iota(jnp.int32, (q, q), 1)
    causal = ri >= ci
    tril = jnp.where(causal, 1.0, 0.0).astype(BF)
    d1, d2, d3 = _split3(da)
    acum = _dot(tril, d1) + _dot(tril, d2) + _dot(tril, d3)
    acum_t = acum.T
    dt_t = dt.T
    a_last = acum[q - 1:q, :]
    e_bf = e_ref[...]
    wexp = _expand(jnp.exp(a_last - acum) * dt, e_bf)
    eexp = _expand(jnp.exp(acum), e_bf)
    lane = lax.broadcasted_iota(jnp.int32, (q, LANES), 1)
    low = lane < SSD_HEAD_DIM

    y_parts = []
    for g in range(SSD_GROUPS):
        gc = slice(g * GROUP_INNER, (g + 1) * GROUP_INNER)
        b_g = conv[:, SSD_INNER + g * D_STATE:SSD_INNER + (g + 1) * D_STATE].astype(BF)
        c0 = SSD_INNER + SSD_GROUPS * D_STATE
        c_g = conv[:, c0 + g * D_STATE:c0 + (g + 1) * D_STATE].astype(BF)
        cb = _dot_nt(c_g, b_g)
        xg = conv[:, gc]
        xg_bf = xg.astype(BF)
        yd = []
        for pr in range(HEADS_PER_GROUP // 2):
            ms = []
            for j in range(2):
                hh = g * HEADS_PER_GROUP + pr * 2 + j
                seg = acum[:, hh:hh + 1] - acum_t[hh:hh + 1, :]
                dec = jnp.exp(jnp.where(causal, seg, NEG))
                ms.append((cb * dec * dt_t[hh:hh + 1, :]).astype(BF))
            mp = jnp.concatenate(ms, axis=1)
            xp = xg_bf[:, pr * LANES:(pr + 1) * LANES]
            zero = jnp.zeros_like(xp)
            xbd = jnp.concatenate([jnp.where(low, xp, zero), jnp.where(low, zero, xp)], axis=0)
            yd.append(_dot(mp, xbd))
        yd = jnp.concatenate(yd, axis=1)
        hg = h_ref[g * GROUP_INNER:(g + 1) * GROUP_INNER, :]
        yoff = _dot_nt(c_g, hg.astype(BF)) * eexp[:, gc]
        y_parts.append(yd + yoff + xg * dsk_ref[:, gc])
        xw = (xg * wexp[:, gc]).astype(BF)
        st = _dot_tn(xw, b_g)
        for k in range(HEADS_PER_GROUP):
            hh = g * HEADS_PER_GROUP + k
            rows = slice(hh * SSD_HEAD_DIM, (hh + 1) * SSD_HEAD_DIM)
            cd = jnp.exp(a_last[:, hh:hh + 1])
            h_ref[rows, :] = h_ref[rows, :] * cd + st[k * SSD_HEAD_DIM:(k + 1) * SSD_HEAD_DIM, :]
    y = jnp.concatenate(y_parts, axis=1)
    z = jnp.concatenate([z0_ref[...], z1_ref[...]], axis=1)
    yn_ref[...] = _gated_rmsnorm(y, z, nw_ref[...]).astype(BF)

    @pl.when(c == pl.num_programs(1) - 1)
    def _():
        hout_ref[...] = h_ref[...]


def _ssd_prompt(proj, gate, cw, cb, dtb, alog, dsk, nw, e_bf, layer):
    nc = SEQ // CHUNK
    vec = lambda w: pl.BlockSpec((None, 1, w), lambda b, c: (layer, 0, 0))
    half = SSD_INNER // 2
    return pl.pallas_call(
        _ssd_prompt_kernel,
        grid=(BATCH, nc),
        in_specs=[
            pl.BlockSpec((CHUNK, CONV_DIM), lambda b, c: (b * nc + c, COL_XBC // CONV_DIM)),
            pl.BlockSpec((CHUNK, half), lambda b, c: (b * nc + c, COL_Z // half)),
            pl.BlockSpec((CHUNK, half), lambda b, c: (b * nc + c, COL_Z // half + 1)),
            pl.BlockSpec((CHUNK, LANES), lambda b, c: (b * nc + c, COL_DT // LANES)),
            pl.BlockSpec((None, CONV_W, CONV_DIM), lambda b, c: (layer, 0, 0)),
            vec(CONV_DIM), vec(LANES), vec(LANES), vec(SSD_INNER), vec(SSD_INNER),
            pl.BlockSpec((LANES, SSD_INNER), lambda b, c: (0, 0)),
        ],
        out_specs=[
            pl.BlockSpec((CHUNK, SSD_INNER), lambda b, c: (b * nc + c, 0)),
            pl.BlockSpec((None, SSD_INNER, D_STATE), lambda b, c: (b, 0, 0)),
        ],
        out_shape=[
            jax.ShapeDtypeStruct((NP, SSD_INNER), BF),
            jax.ShapeDtypeStruct((BATCH, SSD_INNER, D_STATE), F32),
        ],
        scratch_shapes=[
            pltpu.VMEM((CHUNK + SUBLANES, CONV_DIM), F32),
            pltpu.VMEM((SSD_INNER, D_STATE), F32),
        ],
        compiler_params=_cparams(("arbitrary", "arbitrary")),
        name="ssd_prompt",
    )(proj, proj, proj, gate, cw, cb, dtb, alog, dsk, nw, e_bf)


def _ssd_sample_kernel(xbc_ref, z0_ref, z1_ref, dt_ref, cs_ref, h0_ref, cw_ref, cb_ref, dtb_ref, alog_ref,
                       dsk_ref, nw_ref, e_ref, *rest):
    yn_ref, hout_ref = rest[-2:]
    t = DEC_SEQ
    r = SSD_SB * t
    l_idx = lax.broadcasted_iota(jnp.int32, (r, 1), 0) & (t - 1)
    seq_idx = lax.broadcasted_iota(jnp.int32, (r, 1), 0) // t

    def bc(v, s):
        w = v.shape[1]
        v3 = v.reshape(SSD_SB, t, w)[:, s:s + 1, :]
        return jnp.broadcast_to(v3, (SSD_SB, t, w)).reshape(r, w)

    xbc = xbc_ref[...]
    st_rows = cs_ref[...]
    acc = None
    for k in range(CONV_W):
        m = CONV_W - 1 - k
        if m == 0:
            val = xbc
        else:
            cur = pltpu.roll(xbc, m, axis=0)
            back = CONV_W - 1 - m
            stv = st_rows if back == 0 else pltpu.roll(st_rows, r - back, axis=0)
            val = jnp.where(l_idx >= m, cur, stv)
        term = val * cw_ref[k:k + 1, :]
        acc = term if acc is None else acc + term
    conv = _silu(acc + cb_ref[...])

    dt = _softplus(dt_ref[...] + dtb_ref[...])
    a = -jnp.exp(alog_ref[...])
    acum = dt * a
    for sh in (1, 2, 4):
        acum = acum + jnp.where(l_idx >= sh, pltpu.roll(acum, sh, axis=0), 0.0)
    a_last = bc(acum, t - 1)
    e_bf = e_ref[...]
    wexp = _expand(jnp.exp(a_last - acum) * dt, e_bf)
    eexp = _expand(jnp.exp(acum), e_bf)
    cdl = jnp.exp(a_last)

    xs = conv[:, :SSD_INNER]
    bm = conv[:, SSD_INNER:SSD_INNER + SSD_GROUPS * D_STATE]
    cm = conv[:, SSD_INNER + SSD_GROUPS * D_STATE:]
    lane = lax.broadcasted_iota(jnp.int32, (r, LANES), 1)
    grp = lane // HEADS_PER_GROUP

    ms = []
    for s in range(t):
        prod = cm * bc(bm, s)
        cb = jnp.zeros((r, LANES), F32)
        for g in range(SSD_GROUPS):
            cbg = jnp.sum(prod[:, g * D_STATE:(g + 1) * D_STATE], axis=-1, keepdims=True)
            cb = jnp.where(grp == g, cbg, cb)
        dec = jnp.exp(jnp.where(l_idx >= s, acum - bc(acum, s), NEG))
        ms.append(cb * dec * bc(dt, s))
    mexp = _expand(jnp.concatenate(ms, axis=0), e_bf)
    yd = mexp[0:r, :] * bc(xs, 0)
    for s in range(1, t):
        yd = yd + mexp[s * r:(s + 1) * r, :] * bc(xs, s)

    xw = xs * wexp
    c_bf = cm.astype(BF)
    b_bf = bm.astype(BF)
    yoff_parts = []
    for g in range(SSD_GROUPS):
        gc = slice(g * GROUP_INNER, (g + 1) * GROUP_INNER)
        sc = slice(g * D_STATE, (g + 1) * D_STATE)
        hcat = h0_ref[:, gc, :].reshape(SSD_SB * GROUP_INNER, D_STATE)
        full = _dot_nt(c_bf[:, sc], hcat.astype(BF))
        yo = jnp.zeros((r, GROUP_INNER), F32)
        for qi in range(SSD_SB):
            yo = jnp.where(seq_idx == qi, full[:, qi * GROUP_INNER:(qi + 1) * GROUP_INNER], yo)
        yoff_parts.append(yo)
        for qi in range(SSD_SB):
            xq = jnp.where(seq_idx == qi, xw[:, gc], 0.0).astype(BF)
            st = _dot_tn(xq, b_bf[:, sc])
            for k in range(HEADS_PER_GROUP):
                hh = g * HEADS_PER_GROUP + k
                rows = slice(hh * SSD_HEAD_DIM, (hh + 1) * SSD_HEAD_DIM)
                cd = cdl[qi * t:qi * t + 1, hh:hh + 1]
                hout_ref[qi, rows, :] = h0_ref[qi, rows, :] * cd + st[k * SSD_HEAD_DIM:(k + 1) * SSD_HEAD_DIM, :]
    yoff = jnp.concatenate(yoff_parts, axis=1) * eexp
    y = yd + yoff + xs * dsk_ref[...]
    z = jnp.concatenate([z0_ref[...], z1_ref[...]], axis=1)
    yn_ref[...] = _gated_rmsnorm(y, z, nw_ref[...]).astype(BF)


def _ssd_sample(proj, gate, cs_pad, h0_all, cw, cb, dtb, alog, dsk, nw, e_bf, hs_prev, layer):
    r = SSD_SB * DEC_SEQ
    base = NP // r
    half = SSD_INNER // 2
    vec = lambda w: pl.BlockSpec((None, 1, w), lambda i: (layer, 0, 0))
    args = [proj, proj, proj, gate, cs_pad, h0_all, cw, cb, dtb, alog, dsk, nw, e_bf]
    in_specs = [
        pl.BlockSpec((r, CONV_DIM), lambda i: (base + i, COL_XBC // CONV_DIM)),
        pl.BlockSpec((r, half), lambda i: (base + i, COL_Z // half)),
        pl.BlockSpec((r, half), lambda i: (base + i, COL_Z // half + 1)),
        pl.BlockSpec((r, LANES), lambda i: (base + i, COL_DT // LANES)),
        pl.BlockSpec((None, r, CONV_DIM), lambda i: (layer, i, 0)),
        pl.BlockSpec((None, SSD_SB, SSD_INNER, D_STATE), lambda i: (layer, i, 0, 0)),
        pl.BlockSpec((None, CONV_W, CONV_DIM), lambda i: (layer, 0, 0)),
        vec(CONV_DIM), vec(LANES), vec(LANES), vec(SSD_INNER), vec(SSD_INNER),
        pl.BlockSpec((LANES, SSD_INNER), lambda i: (0, 0)),
    ]
    aliases = {}
    if hs_prev is not None:
        aliases[len(args)] = 1
        args.append(hs_prev)
        in_specs.append(pl.BlockSpec(memory_space=pl.ANY))
    return pl.pallas_call(
        _ssd_sample_kernel,
        grid=(DEC_BATCH // SSD_SB,),
        in_specs=in_specs,
        out_specs=[
            pl.BlockSpec((r, SSD_INNER), lambda i: (i, 0)),
            pl.BlockSpec((None, SSD_SB, SSD_INNER, D_STATE), lambda i: (layer, i, 0, 0)),
        ],
        out_shape=[
            jax.ShapeDtypeStruct((NS, SSD_INNER), BF),
            jax.ShapeDtypeStruct((DEPTH, DEC_BATCH, SSD_INNER, D_STATE), F32),
        ],
        input_output_aliases=aliases,
        compiler_params=_cparams(("arbitrary",)),
        name="ssd_sample",
    )(*args)


def _mix_kernel(ap_ref, as_ref, ynp_ref, yns_ref, ga_ref, gb_ref, x_ref, wpo_ref, wso_ref, wo_ref, gab_ref,
                gbb_ref, g1_ref, b1_ref, wr_ref, rb_ref, x1_ref, rt_ref):
    is_prompt = pl.program_id(0) < NP // MIX_TM
    a = jnp.where(is_prompt, ap_ref[...], as_ref[...])
    yn = jnp.where(is_prompt, ynp_ref[...], yns_ref[...])
    pa = _dot(a, wpo_ref[...])
    ps = _dot(yn, wso_ref[...])
    m = jax.nn.sigmoid(ga_ref[...] + gab_ref[...]) * pa + jax.nn.sigmoid(gb_ref[...] + gbb_ref[...]) * ps
    res = DN_ALPHA * x_ref[...] + _dot(m.astype(BF), wo_ref[...])
    x1 = _layer_norm(res, g1_ref[...], b1_ref[...])
    x1_ref[...] = x1

    xh, xl = _split2(x1)
    wh, wl = _split2(wr_ref[...])
    logits = _dot(xh, wh) + _dot(xh, wl) + _dot(xl, wh) + rb_ref[...]
    tm = logits.shape[0]
    lane = lax.broadcasted_iota(jnp.int32, (tm, LANES), 1)
    big = 4 * LANES
    is_g = (lane >= N_EXPERTS) & (lane < N_EXPERTS + N_EXPERT_GROUPS)
    gl = jnp.where(is_g, logits, NEG)
    gmax = jnp.max(gl, axis=-1, keepdims=True)
    gidx = jnp.min(jnp.where(gl == gmax, lane, big), axis=-1, keepdims=True) - N_EXPERTS
    gsum = jnp.sum(jnp.where(is_g, jnp.exp(gl - gmax), 0.0), axis=-1, keepdims=True)
    gval = 1.0 / gsum
    in_grp = (lane < N_EXPERTS) & ((lane // EXPERTS_PER_GROUP) == gidx)
    el = jnp.where(in_grp, logits, NEG)
    m1 = jnp.max(el, axis=-1, keepdims=True)
    i1 = jnp.min(jnp.where(el == m1, lane, big), axis=-1, keepdims=True)
    el2 = jnp.where(lane == i1, NEG, el)
    m2 = jnp.max(el2, axis=-1, keepdims=True)
    i2 = jnp.min(jnp.where(el2 == m2, lane, big), axis=-1, keepdims=True)
    r21 = jnp.exp(m2 - m1)
    w1 = gval / (1.0 + r21)
    w2 = w1 * r21
    rt = jnp.where(lane == 0, i1.astype(F32),
                   jnp.where(lane == 1, i2.astype(F32),
                             jnp.where(lane == 2, w1, jnp.where(lane == 3, w2, 0.0))))
    rt_ref[...] = rt


def _mix(a_p, a_s, yn_p, yn_s, gate, x, wpo, wso, wo, gab, gbb, g1, b1, wr, rb, layer):
    const = lambda shape: pl.BlockSpec((None,) + shape, lambda i: (layer,) + (0,) * len(shape),
                                       pipeline_mode=pl.Buffered(1))
    vec = lambda w: pl.BlockSpec((None, 1, w), lambda i: (layer, 0, 0))
    n_p = NP // MIX_TM
    prompt_blk = lambda i: (jnp.minimum(i, n_p - 1), 0)
    sample_blk = lambda i: (jnp.maximum(i - n_p, 0), 0)
    return pl.pallas_call(
        _mix_kernel,
        grid=(NTOK // MIX_TM,),
        in_specs=[
            pl.BlockSpec((MIX_TM, POOL_DIM), prompt_blk),
            pl.BlockSpec((MIX_TM, POOL_DIM), sample_blk),
            pl.BlockSpec((MIX_TM, SSD_INNER), prompt_blk),
            pl.BlockSpec((MIX_TM, SSD_INNER), sample_blk),
            pl.BlockSpec((MIX_TM, D_MODEL), lambda i: (i, COL_GA // D_MODEL)),
            pl.BlockSpec((MIX_TM, D_MODEL), lambda i: (i, COL_GB // D_MODEL)),
            pl.BlockSpec((MIX_TM, D_MODEL), lambda i: (i, 0)),
            const((POOL_DIM, D_MODEL)), const((SSD_INNER, D_MODEL)), const((D_MODEL, D_MODEL)),
            vec(D_MODEL), vec(D_MODEL), vec(D_MODEL), vec(D_MODEL),
            const((D_MODEL, LANES)), vec(LANES),
        ],
        out_specs=[
            pl.BlockSpec((MIX_TM, D_MODEL), lambda i: (i, 0)),
            pl.BlockSpec((MIX_TM, LANES), lambda i: (i, 0)),
        ],
        out_shape=[
            jax.ShapeDtypeStruct((NTOK, D_MODEL), F32),
            jax.ShapeDtypeStruct((NTOK, LANES), F32),
        ],
        compiler_params=_cparams(("arbitrary",)),
        name="mix",
    )(a_p, a_s, yn_p, yn_s, gate, gate, x, wpo, wso, wo, gab, gbb, g1, b1, wr, rb)


def _route_tables(rt):
    e = jnp.concatenate([rt[:, 0], rt[:, 1]]).astype(jnp.int32)
    oh = (e[:, None] == jnp.arange(N_EXPERTS, dtype=jnp.int32)[None, :]).astype(jnp.int32)
    cs = jnp.cumsum(oh, axis=0)
    rank = jnp.sum((cs - oh) * oh, axis=1)
    counts = cs[-1]
    padded = ((counts + MOE_T - 1) // MOE_T) * MOE_T
    ends = jnp.cumsum(padded)
    off = ends - padded
    pos = (jnp.take(off, e) + rank).astype(jnp.int32)
    n_used = (ends[-1] // MOE_T).astype(jnp.int32)
    tiles = jnp.arange(MOE_TILES, dtype=jnp.int32)
    tile_blk = jnp.minimum(tiles, n_used - 1)
    tile_e = jnp.sum((ends[None, :] // MOE_T <= tile_blk[:, None]).astype(jnp.int32), axis=1)
    tile_e = jnp.minimum(tile_e, N_EXPERTS - 1)
    pad_lo = (ends - MOE_T).astype(jnp.int32)
    return pos, tile_blk, tile_e, n_used.reshape(1), pad_lo, counts.astype(jnp.int32)


def _row_copy(src, dst, si, di, sem):
    return pltpu.make_async_copy(src.at[pl.ds(si, 1)], dst.at[pl.ds(di, 1)], sem)


def _dispatch_kernel(pos_ref, padlo_ref, cnt_ref, x_ref, xs_hbm, zero_ref, sem):
    i = pl.program_id(0)

    @pl.when(i == 0)
    def _():
        zero_ref[...] = jnp.zeros_like(zero_ref)

        def zero_copy(e):
            lo = pl.multiple_of(padlo_ref[e], MOE_T)
            return pltpu.make_async_copy(zero_ref, xs_hbm.at[pl.ds(lo, MOE_T)], sem)

        def zstart(e, c):
            @pl.when(cnt_ref[e] > 0)
            def _():
                zero_copy(e).start()
            return c

        def zwait(e, c):
            @pl.when(cnt_ref[e] > 0)
            def _():
                zero_copy(e).wait()
            return c

        lax.fori_loop(0, N_EXPERTS, zstart, 0)
        lax.fori_loop(0, N_EXPERTS, zwait, 0)

    t0 = i * DISP_TD

    def start(j, c):
        t = t0 + j
        _row_copy(x_ref, xs_hbm, j, pos_ref[t], sem).start()
        _row_copy(x_ref, xs_hbm, j, pos_ref[NTOK + t], sem).start()
        return c

    lax.fori_loop(0, DISP_TD, start, 0, unroll=8)
    for _ in range(2):
        pltpu.make_async_copy(x_ref, xs_hbm.at[pl.ds(0, DISP_TD)], sem).wait()


def _dispatch(pos, pad_lo, counts, x1):
    return pl.pallas_call(
        _dispatch_kernel,
        grid_spec=pltpu.PrefetchScalarGridSpec(
            num_scalar_prefetch=3,
            grid=(NTOK // DISP_TD,),
            in_specs=[pl.BlockSpec((DISP_TD, D_MODEL), lambda i, pos, lo, cnt: (i, 0))],
            out_specs=pl.BlockSpec(memory_space=pl.ANY),
            scratch_shapes=[pltpu.VMEM((MOE_T, D_MODEL), F32), pltpu.SemaphoreType.DMA(())],
        ),
        out_shape=jax.ShapeDtypeStruct((MOE_ROWS, D_MODEL), F32),
        compiler_params=_cparams(("arbitrary",)),
        name="moe_dispatch",
    )(pos, pad_lo, counts, x1)


def _expert_kernel(blk_ref, te_ref, nu_ref, x_ref, wg_ref, wu_ref, wd_ref, o_ref, wg_bf, wu_bf, wd_bf):
    i = pl.program_id(0)
    prev = te_ref[jnp.maximum(i - 1, 0)]

    @pl.when((i == 0) | (te_ref[i] != prev))
    def _():
        wg_bf[...] = wg_ref[...].astype(BF)
        wu_bf[...] = wu_ref[...].astype(BF)
        wd_bf[...] = wd_ref[...].astype(BF)

    @pl.when(i < nu_ref[0])
    def _():
        x = x_ref[...].astype(BF)
        h = _silu(_dot(x, wg_bf[...])) * _dot(x, wu_bf[...])
        o_ref[...] = _dot(h.astype(BF), wd_bf[...])


def _experts(tile_blk, tile_e, n_used, xs, wg, wu, wd, layer):
    wspec = lambda a, b: pl.BlockSpec((None, None, a, b), lambda i, blk, te, nu: (layer, te[i], 0, 0))
    return pl.pallas_call(
        _expert_kernel,
        grid_spec=pltpu.PrefetchScalarGridSpec(
            num_scalar_prefetch=3,
            grid=(MOE_TILES,),
            in_specs=[
                pl.BlockSpec((MOE_T, D_MODEL), lambda i, blk, te, nu: (blk[i], 0)),
                wspec(D_MODEL, D_EXPERT), wspec(D_MODEL, D_EXPERT), wspec(D_EXPERT, D_MODEL),
            ],
            out_specs=pl.BlockSpec((MOE_T, D_MODEL), lambda i, blk, te, nu: (blk[i], 0)),
            scratch_shapes=[
                pltpu.VMEM((D_MODEL, D_EXPERT), BF),
                pltpu.VMEM((D_MODEL, D_EXPERT), BF),
                pltpu.VMEM((D_EXPERT, D_MODEL), BF),
            ],
        ),
        out_shape=jax.ShapeDtypeStruct((MOE_ROWS, D_MODEL), F32),
        compiler_params=_cparams(("arbitrary",)),
        name="moe_experts",
    )(tile_blk, tile_e, n_used, xs, wg, wu, wd)


def _combine_kernel(pos_ref, ys_hbm, x1_ref, rt_ref, g2_ref, b2_ref, x2_ref, xbf_ref, buf_ref, sem):
    i = pl.program_id(0)
    t0 = i * COMB_TC

    def start(j, c):
        t = t0 + j
        _row_copy(ys_hbm, buf_ref.at[0], pos_ref[t], j, sem).start()
        _row_copy(ys_hbm, buf_ref.at[1], pos_ref[NTOK + t], j, sem).start()
        return c

    lax.fori_loop(0, COMB_TC, start, 0, unroll=8)
    for slot in range(2):
        pltpu.make_async_copy(ys_hbm.at[pl.ds(0, COMB_TC)], buf_ref.at[slot], sem).wait()
    rt = rt_ref[...]
    f = rt[:, 2:3] * buf_ref[0] + rt[:, 3:4] * buf_ref[1]
    x2 = _layer_norm(DN_ALPHA * x1_ref[...] + f, g2_ref[...], b2_ref[...])
    x2_ref[...] = x2
    xbf_ref[...] = x2.astype(BF)


def _combine(pos, ys, x1, rt, g2, b2, layer):
    vec = lambda w: pl.BlockSpec((None, 1, w), lambda i, pos: (layer, 0, 0))
    return pl.pallas_call(
        _combine_kernel,
        grid_spec=pltpu.PrefetchScalarGridSpec(
            num_scalar_prefetch=1,
            grid=(NTOK // COMB_TC,),
            in_specs=[
                pl.BlockSpec(memory_space=pl.ANY),
                pl.BlockSpec((COMB_TC, D_MODEL), lambda i, pos: (i, 0)),
                pl.BlockSpec((COMB_TC, LANES), lambda i, pos: (i, 0)),
                vec(D_MODEL), vec(D_MODEL),
            ],
            out_specs=[
                pl.BlockSpec((COMB_TC, D_MODEL), lambda i, pos: (i, 0)),
                pl.BlockSpec((COMB_TC, D_MODEL), lambda i, pos: (i, 0)),
            ],
            scratch_shapes=[pltpu.VMEM((2, COMB_TC, D_MODEL), F32), pltpu.SemaphoreType.DMA(())],
        ),
        out_shape=[
            jax.ShapeDtypeStruct((NTOK, D_MODEL), F32),
            jax.ShapeDtypeStruct((NTOK, D_MODEL), BF),
        ],
        compiler_params=_cparams(("arbitrary",)),
        name="moe_combine",
    )(pos, ys, x1, rt, g2, b2)


def kernel(x_prompt, x_sample, state_ssm, state_conv, state_pool, w_in, pool_map_w, pool_map_b, pool_scale,
           conv_w, conv_b, dt_bias, a_log, d_skip, ssd_norm_w, gate_a_bias, gate_b_bias, w_pool_out, w_ssd_out,
           w_o, ln1_g, ln1_b, router_group_w, router_group_b, router_expert_w, router_expert_b, expert_w_gate,
           expert_w_up, expert_w_down, ln2_g, ln2_b):
    dt0, ga0, gb0 = SPLIT_POINTS[2], SPLIT_POINTS[3], SPLIT_POINTS[4]
    wgate = jnp.concatenate([w_in[:, :, ga0:gb0].astype(BF), w_in[:, :, gb0:].astype(BF),
                             w_in[:, :, dt0:ga0].astype(BF),
                             jnp.zeros((DEPTH, D_MODEL, LANES - SSD_HEADS), BF)], axis=-1)
    wpo = w_pool_out.astype(BF)
    wso = w_ssd_out.astype(BF)
    wo = w_o.astype(BF)
    r_pad = jnp.zeros((DEPTH, D_MODEL, LANES - N_EXPERTS - N_EXPERT_GROUPS), F32)
    wr = jnp.concatenate([router_expert_w, router_group_w, r_pad], axis=-1)
    rb = jnp.concatenate([router_expert_b, router_group_b,
                          jnp.zeros((DEPTH, LANES - N_EXPERTS - N_EXPERT_GROUPS), F32)], axis=-1)[:, None, :]
    row = lambda v: v[:, None, :]
    head_pad = lambda v: jnp.pad(v, ((0, 0), (0, LANES - SSD_HEADS)))[:, None, :]
    dtb = head_pad(dt_bias)
    alog = head_pad(a_log)
    dsk = row(jnp.repeat(d_skip, SSD_HEAD_DIM, axis=-1))
    e_bf = (jnp.arange(SSD_INNER)[None, :] // SSD_HEAD_DIM == jnp.arange(LANES)[:, None]).astype(BF)
    wg = expert_w_gate.reshape(DEPTH, N_EXPERTS, D_MODEL, D_EXPERT)
    wu = expert_w_up.reshape(DEPTH, N_EXPERTS, D_MODEL, D_EXPERT)
    wd = expert_w_down.reshape(DEPTH, N_EXPERTS, D_EXPERT, D_MODEL)
    cs_pad = jnp.pad(state_conv, ((0, 0), (0, 0), (0, DEC_SEQ - (CONV_W - 1)), (0, 0)))
    cs_pad = cs_pad.reshape(DEPTH, NS, CONV_DIM)
    h0_all = state_ssm.reshape(DEPTH, DEC_BATCH, SSD_INNER, D_STATE)

    x = jnp.concatenate([x_prompt.reshape(NP, D_MODEL), x_sample.reshape(NS, D_MODEL)], axis=0)
    x_bf = x.astype(BF)

    ssm_p, conv_p, pool_p, conv_s, pool_s = [], [], [], [], []
    hs_all = None
    for l in range(DEPTH):
        proj = _inproj_main(x_bf, w_in, l)
        gate = _inproj_gate(x_bf, wgate, l)
        tails = [lax.slice(proj, (b * SEQ + SEQ - POOL_BUF, 0), ((b + 1) * SEQ, MAIN_W)) for b in range(BATCH)]
        tails = jnp.stack(tails, 0)
        pool_p.append(tails[:, :, COL_U:COL_U + POOL_DIM])
        conv_p.append(tails[:, POOL_BUF - (CONV_W - 1):, COL_XBC:])
        proj_s = lax.slice(proj, (NP, 0), (NTOK, MAIN_W)).reshape(DEC_BATCH, DEC_SEQ, MAIN_W)
        u_s = proj_s[:, :, COL_U:COL_U + POOL_DIM]
        pool_s.append(jnp.concatenate([state_pool[l][:, DEC_SEQ:], u_s], axis=1))
        conv_s.append(proj_s[:, DEC_SEQ - (CONV_W - 1):, COL_XBC:])

        pmb = row(pool_map_b.reshape(DEPTH, POOL_DIM))
        a_p = _pool_prompt(proj, pool_map_w, pmb, row(pool_scale), l)
        ext_s = jnp.concatenate([jnp.zeros((DEC_BATCH, 1, POOL_DIM), F32), state_pool[l], u_s], axis=1)
        a_s = _pool_sample(ext_s.reshape(DEC_BATCH * POOL_SLAB, POOL_DIM), pool_map_w, pmb, row(pool_scale), l)
        yn_p, h_p = _ssd_prompt(proj, gate, conv_w, row(conv_b), dtb, alog, dsk, row(ssd_norm_w), e_bf, l)
        ssm_p.append(h_p)
        yn_s, hs_all = _ssd_sample(proj, gate, cs_pad, h0_all, conv_w, row(conv_b), dtb, alog, dsk,
                                   row(ssd_norm_w), e_bf, hs_all, l)
        x1, rt = _mix(a_p, a_s, yn_p, yn_s, gate, x, wpo, wso, wo, row(gate_a_bias), row(gate_b_bias),
                      row(ln1_g), row(ln1_b), wr, rb, l)
        pos, tile_blk, tile_e, n_used, pad_lo, counts = _route_tables(rt)
        xs = _dispatch(pos, pad_lo, counts, x1)
        ys = _experts(tile_blk, tile_e, n_used, xs, wg, wu, wd, l)
        x, x_bf = _combine(pos, ys, x1, rt, row(ln2_g), row(ln2_b), l)

    y_prompt = x[:NP].reshape(BATCH, SEQ, D_MODEL)
    y_sample = x[NP:].reshape(DEC_BATCH, DEC_SEQ, D_MODEL)
    new_ssm_prompt = jnp.stack(ssm_p, 0).reshape(DEPTH, BATCH, SSD_HEADS, SSD_HEAD_DIM, D_STATE)
    new_ssm_sample = hs_all.reshape(DEPTH, DEC_BATCH, SSD_HEADS, SSD_HEAD_DIM, D_STATE)
    return (y_prompt, y_sample, new_ssm_prompt, jnp.stack(conv_p, 0), jnp.stack(pool_p, 0),
            new_ssm_sample, jnp.stack(conv_s, 0), jnp.stack(pool_s, 0))
```

```python
import jax
import jax.numpy as jnp
from jax import lax
from jax.experimental import pallas as pl
from jax.experimental.pallas import tpu as pltpu

F32 = jnp.float32
BF = jnp.bfloat16

D_MODEL = 2048
BATCH = 4
SEQ = 2048
DEPTH = 4
DEC_BATCH = 128
DEC_SEQ = 8
PAST_LEN = 16384
POOL_WINDOWS = (2, 4, 8, 16)
N_POOL_GROUPS = 4
POOL_DIM = D_MODEL // 2
POOL_GROUP_DIM = POOL_DIM // N_POOL_GROUPS
POOL_BUF = max(POOL_WINDOWS) - 1
SSD_INNER = D_MODEL
SSD_HEAD_DIM = 64
SSD_HEADS = SSD_INNER // SSD_HEAD_DIM
SSD_GROUPS = 4
HEADS_PER_GROUP = SSD_HEADS // SSD_GROUPS
GROUP_INNER = SSD_INNER // SSD_GROUPS
D_STATE = 128
CONV_W = 4
CONV_DIM = SSD_INNER + 2 * SSD_GROUPS * D_STATE
CHUNK = 128
N_EXPERT_GROUPS = 4
EXPERTS_PER_GROUP = 8
N_EXPERTS = N_EXPERT_GROUPS * EXPERTS_PER_GROUP
D_EXPERT = D_MODEL // 4
DN_ALPHA = (2.0 * DEPTH) ** 0.25
LN_EPS = 1e-5
RMS_EPS = 1e-5
SPLIT_POINTS = (1024, 3072, 6144, 6176, 8224)

NP = BATCH * SEQ
NS = DEC_BATCH * DEC_SEQ
NTOK = NP + NS

LANES = 128
SUBLANES = 8
VMEM_LIMIT = 56 * 1024 * 1024

COL_U = 0
COL_Z = POOL_DIM
COL_XBC = POOL_DIM + SSD_INNER
COL_GA = POOL_DIM + SSD_INNER + CONV_DIM
COL_GB = COL_GA + D_MODEL
PROJ_W = COL_GB + D_MODEL
W_DT = SPLIT_POINTS[2]
W_GA = SPLIT_POINTS[3]

NEG = -1e30

PROJ_TM = 1024
PROJ_TN = 1024
POOL_TM = 512
POOL_SB = 16
POOL_SLAB = 24
SSD_SB = 4
MIX_TM = 256
MOE_T = 256
MOE_TILES = 2 * NTOK // MOE_T + N_EXPERTS
MOE_ROWS = MOE_TILES * MOE_T
DISP_TD = 512
COMB_TC = 256


def _cparams(sem, vmem=VMEM_LIMIT):
    return pltpu.CompilerParams(dimension_semantics=sem, vmem_limit_bytes=vmem)


def _dot(a, b):
    return jnp.dot(a, b, preferred_element_type=F32)


def _dot_nt(a, b):
    return lax.dot_general(a, b, (((1,), (1,)), ((), ())), preferred_element_type=F32)


def _dot_tn(a, b):
    return lax.dot_general(a, b, (((0,), (0,)), ((), ())), preferred_element_type=F32)


def _split2(v):
    hi = v.astype(BF)
    lo = (v - hi.astype(F32)).astype(BF)
    return hi, lo


def _split3(v):
    h1 = v.astype(BF)
    r1 = v - h1.astype(F32)
    h2 = r1.astype(BF)
    h3 = (r1 - h2.astype(F32)).astype(BF)
    return h1, h2, h3


def _expand(v, e_bf):
    hi, lo = _split2(v)
    return _dot(hi, e_bf) + _dot(lo, e_bf)


def _silu(x):
    return x * jax.nn.sigmoid(x)


def _softplus(x):
    return jnp.maximum(x, 0.0) + jnp.log1p(jnp.exp(-jnp.abs(x)))


def _layer_norm(x, g, b):
    mu = jnp.mean(x, axis=-1, keepdims=True)
    xc = x - mu
    var = jnp.mean(xc * xc, axis=-1, keepdims=True)
    return xc * lax.rsqrt(var + LN_EPS) * g + b


def _gated_rmsnorm(y, z, nw):
    v = y * _silu(z)
    outs = []
    for g in range(SSD_GROUPS):
        vg = v[:, g * GROUP_INNER:(g + 1) * GROUP_INNER]
        ms = jnp.mean(vg * vg, axis=-1, keepdims=True)
        outs.append(vg * lax.rsqrt(ms + RMS_EPS))
    return jnp.concatenate(outs, axis=-1) * nw


def _inproj_kernel(x_ref, wt_ref, o_ref, wbf_ref):
    @pl.when(pl.program_id(1) == 0)
    def _():
        wbf_ref[...] = wt_ref[0].astype(BF)

    o_ref[...] = _dot_nt(x_ref[...], wbf_ref[...])


def _inproj(x_bf, w_in_t, layer):
    n_main = COL_GA // PROJ_TN

    def w_rows(j, i):
        row0 = j * PROJ_TN + jnp.where(j >= n_main, W_GA - W_DT, 0)
        return (layer, pl.multiple_of(row0, W_GA - W_DT), 0)

    return pl.pallas_call(
        _inproj_kernel,
        grid=(PROJ_W // PROJ_TN, NTOK // PROJ_TM),
        in_specs=[
            pl.BlockSpec((PROJ_TM, D_MODEL), lambda j, i: (i, 0)),
            pl.BlockSpec((pl.Element(1), pl.Element(PROJ_TN), pl.Element(D_MODEL)), w_rows),
        ],
        out_specs=pl.BlockSpec((PROJ_TM, PROJ_TN), lambda j, i: (i, j)),
        out_shape=jax.ShapeDtypeStruct((NTOK, PROJ_W), F32),
        scratch_shapes=[pltpu.VMEM((PROJ_TN, D_MODEL), BF)],
        compiler_params=_cparams(("arbitrary", "arbitrary")),
        name="inproj",
    )(x_bf, w_in_t)


def _pool_group(eg, lead, u_rows, pos, win, mw, mb, sc, take):
    s = eg
    sh = 1
    while sh < win:
        s = s + pltpu.roll(s, sh, axis=0)
        sh *= 2
    s = take(s)
    cnt = jnp.minimum(pos + 1, win).astype(F32)
    d = s / cnt - u_rows
    mixed = _dot(d.astype(BF), mw.astype(BF)) + mb
    return mixed * sc


def _pool_prompt_kernel(u_ref, mw_ref, mb_ref, sc_ref, o_ref, ext_ref):
    r = pl.program_id(1)
    halo = 2 * SUBLANES

    @pl.when(r == 0)
    def _():
        ext_ref[0:halo, :] = jnp.zeros((halo, POOL_DIM), F32)

    u = u_ref[...]
    ext_ref[halo:halo + POOL_TM, :] = u
    e = ext_ref[...]
    pos = r * POOL_TM + lax.broadcasted_iota(jnp.int32, (POOL_TM, 1), 0)
    for g, win in enumerate(POOL_WINDOWS):
        cs = slice(g * POOL_GROUP_DIM, (g + 1) * POOL_GROUP_DIM)
        out = _pool_group(e[:, cs], halo, u[:, cs], pos, win, mw_ref[g], mb_ref[:, cs], sc_ref[:, cs],
                          lambda s: s[halo:, :])
        o_ref[:, cs] = out.astype(BF)
    ext_ref[0:halo, :] = u[POOL_TM - halo:, :]


def _pool_prompt(proj, mw, mb, sc, layer):
    rt = SEQ // POOL_TM
    return pl.pallas_call(
        _pool_prompt_kernel,
        grid=(BATCH, rt),
        in_specs=[
            pl.BlockSpec((POOL_TM, POOL_DIM), lambda b, r: (b * rt + r, COL_U // POOL_DIM)),
            pl.BlockSpec((None, N_POOL_GROUPS, POOL_GROUP_DIM, POOL_GROUP_DIM), lambda b, r: (layer, 0, 0, 0)),
            pl.BlockSpec((None, 1, POOL_DIM), lambda b, r: (layer, 0, 0)),
            pl.BlockSpec((None, 1, POOL_DIM), lambda b, r: (layer, 0, 0)),
        ],
        out_specs=pl.BlockSpec((POOL_TM, POOL_DIM), lambda b, r: (b * rt + r, 0)),
        out_shape=jax.ShapeDtypeStruct((NP, POOL_DIM), BF),
        scratch_shapes=[pltpu.VMEM((POOL_TM + 2 * SUBLANES, POOL_DIM), F32)],
        compiler_params=_cparams(("arbitrary", "arbitrary")),
        name="pool_prompt",
    )(proj, mw, mb, sc)


def _pool_sample_kernel(ext_ref, mw_ref, mb_ref, sc_ref, o_ref):
    e = ext_ref[...]
    rows = POOL_SB * DEC_SEQ
    first = POOL_SLAB - DEC_SEQ
    pos = PAST_LEN + (lax.broadcasted_iota(jnp.int32, (rows, 1), 0) & (DEC_SEQ - 1))

    def take(s):
        s3 = s.reshape(POOL_SB, POOL_SLAB, POOL_GROUP_DIM)[:, first:, :]
        return s3.reshape(rows, POOL_GROUP_DIM)

    for g, win in enumerate(POOL_WINDOWS):
        cs = slice(g * POOL_GROUP_DIM, (g + 1) * POOL_GROUP_DIM)
        eg = e[:, cs]
        out = _pool_group(eg, first, take(eg), pos, win, mw_ref[g], mb_ref[:, cs], sc_ref[:, cs], take)
        o_ref[:, cs] = out.astype(BF)


def _pool_sample(ext_s, mw, mb, sc, layer):
    rows = POOL_SB * DEC_SEQ
    return pl.pallas_call(
        _pool_sample_kernel,
        grid=(DEC_BATCH // POOL_SB,),
        in_specs=[
            pl.BlockSpec((POOL_SB * POOL_SLAB, POOL_DIM), lambda i: (i, 0)),
            pl.BlockSpec((None, N_POOL_GROUPS, POOL_GROUP_DIM, POOL_GROUP_DIM), lambda i: (layer, 0, 0, 0)),
            pl.BlockSpec((None, 1, POOL_DIM), lambda i: (layer, 0, 0)),
            pl.BlockSpec((None, 1, POOL_DIM), lambda i: (layer, 0, 0)),
        ],
        out_specs=pl.BlockSpec((rows, POOL_DIM), lambda i: (i, 0)),
        out_shape=jax.ShapeDtypeStruct((NS, POOL_DIM), BF),
        compiler_params=_cparams(("arbitrary",)),
        name="pool_sample",
    )(ext_s, mw, mb, sc)


def _dt_heads(xbf_ref, wdt_ref, dtb_ref):
    raw = _dot_nt(xbf_ref[...], wdt_ref[...].astype(BF))
    lane = lax.broadcasted_iota(jnp.int32, raw.shape, 1)
    return _softplus(jnp.where(lane < SSD_HEADS, raw, 0.0) + dtb_ref[...])


def _ssd_prompt_kernel(xbc_ref, z0_ref, z1_ref, xbf_ref, wdt_ref, cw_ref, cb_ref, dtb_ref, alog_ref, dsk_ref,
                       nw_ref, e_ref, yn_ref, hout_ref, ext_ref, h_ref):
    c = pl.program_id(1)
    q = CHUNK
    halo = SUBLANES

    @pl.when(c == 0)
    def _():
        ext_ref[0:halo, :] = jnp.zeros((halo, CONV_DIM), F32)
        h_ref[...] = jnp.zeros_like(h_ref)

    xbc = xbc_ref[...]
    ext_ref[halo:halo + q, :] = xbc
    first = halo - (CONV_W - 1)
    acc = ext_ref[first:first + q, :] * cw_ref[0:1, :]
    for k in range(1, CONV_W):
        acc = acc + ext_ref[first + k:first + k + q, :] * cw_ref[k:k + 1, :]
    ext_ref[0:halo, :] = xbc[q - halo:, :]
    conv = _silu(acc + cb_ref[...])

    dt = _dt_heads(xbf_ref, wdt_ref, dtb_ref)
    a = -jnp.exp(alog_ref[...])
    da = dt * a
    ri = lax.broadcasted_iota(jnp.int32, (q, q), 0)
    ci = lax.broadcasted_iota(jnp.int32, (q, q), 1)
    causal = ri >= ci
    tril = jnp.where(causal, 1.0, 0.0).astype(BF)
    d1, d2, d3 = _split3(da)
    acum = _dot(tril, d1) + _dot(tril, d2) + _dot(tril, d3)
    acum_t = acum.T
    dt_t = dt.T
    a_last = acum[q - 1:q, :]
    e_bf = e_ref[...]
    wexp = _expand(jnp.exp(a_last - acum) * dt, e_bf)
    eexp = _expand(jnp.exp(acum), e_bf)
    lane = lax.broadcasted_iota(jnp.int32, (q, LANES), 1)
    low = lane < SSD_HEAD_DIM

    y_parts = []
    for g in range(SSD_GROUPS):
        gc = slice(g * GROUP_INNER, (g + 1) * GROUP_INNER)
        b_g = conv[:, SSD_INNER + g * D_STATE:SSD_INNER + (g + 1) * D_STATE].astype(BF)
        c0 = SSD_INNER + SSD_GROUPS * D_STATE
        c_g = conv[:, c0 + g * D_STATE:c0 + (g + 1) * D_STATE].astype(BF)
        cb = _dot_nt(c_g, b_g)
        xg = conv[:, gc]
        xg_bf = xg.astype(BF)
        yd = []
        for pr in range(HEADS_PER_GROUP // 2):
            ms = []
            for j in range(2):
                hh = g * HEADS_PER_GROUP + pr * 2 + j
                seg = acum[:, hh:hh + 1] - acum_t[hh:hh + 1, :]
                dec = jnp.exp(jnp.where(causal, seg, NEG))
                ms.append((cb * dec * dt_t[hh:hh + 1, :]).astype(BF))
            mp = jnp.concatenate(ms, axis=1)
            xp = xg_bf[:, pr * LANES:(pr + 1) * LANES]
            zero = jnp.zeros_like(xp)
            xbd = jnp.concatenate([jnp.where(low, xp, zero), jnp.where(low, zero, xp)], axis=0)
            yd.append(_dot(mp, xbd))
        yd = jnp.concatenate(yd, axis=1)
        hg = h_ref[g * GROUP_INNER:(g + 1) * GROUP_INNER, :]
        yoff = _dot_nt(c_g, hg.astype(BF)) * eexp[:, gc]
        y_parts.append(yd + yoff + xg * dsk_ref[:, gc])
        xw = (xg * wexp[:, gc]).astype(BF)
        st = _dot_tn(xw, b_g)
        for k in range(HEADS_PER_GROUP):
            hh = g * HEADS_PER_GROUP + k
            rows = slice(hh * SSD_HEAD_DIM, (hh + 1) * SSD_HEAD_DIM)
            cd = jnp.exp(a_last[:, hh:hh + 1])
            h_ref[rows, :] = h_ref[rows, :] * cd + st[k * SSD_HEAD_DIM:(k + 1) * SSD_HEAD_DIM, :]
    y = jnp.concatenate(y_parts, axis=1)
    z = jnp.concatenate([z0_ref[...], z1_ref[...]], axis=1)
    yn_ref[...] = _gated_rmsnorm(y, z, nw_ref[...]).astype(BF)

    @pl.when(c == pl.num_programs(1) - 1)
    def _():
        hout_ref[...] = h_ref[...]


def _ssd_prompt(proj, x_bf, w_in_t, cw, cb, dtb, alog, dsk, nw, e_bf, layer):
    nc = SEQ // CHUNK
    vec = lambda w: pl.BlockSpec((None, 1, w), lambda b, c: (layer, 0, 0))
    half = SSD_INNER // 2
    return pl.pallas_call(
        _ssd_prompt_kernel,
        grid=(BATCH, nc),
        in_specs=[
            pl.BlockSpec((CHUNK, CONV_DIM), lambda b, c: (b * nc + c, COL_XBC // CONV_DIM)),
            pl.BlockSpec((CHUNK, half), lambda b, c: (b * nc + c, COL_Z // half)),
            pl.BlockSpec((CHUNK, half), lambda b, c: (b * nc + c, COL_Z // half + 1)),
            pl.BlockSpec((CHUNK, D_MODEL), lambda b, c: (b * nc + c, 0)),
            pl.BlockSpec((None, LANES, D_MODEL), lambda b, c: (layer, W_DT // LANES, 0)),
            pl.BlockSpec((None, CONV_W, CONV_DIM), lambda b, c: (layer, 0, 0)),
            vec(CONV_DIM), vec(LANES), vec(LANES), vec(SSD_INNER), vec(SSD_INNER),
            pl.BlockSpec((LANES, SSD_INNER), lambda b, c: (0, 0)),
        ],
        out_specs=[
            pl.BlockSpec((CHUNK, SSD_INNER), lambda b, c: (b * nc + c, 0)),
            pl.BlockSpec((None, SSD_INNER, D_STATE), lambda b, c: (b, 0, 0)),
        ],
        out_shape=[
            jax.ShapeDtypeStruct((NP, SSD_INNER), BF),
            jax.ShapeDtypeStruct((BATCH, SSD_INNER, D_STATE), F32),
        ],
        scratch_shapes=[
            pltpu.VMEM((CHUNK + SUBLANES, CONV_DIM), F32),
            pltpu.VMEM((SSD_INNER, D_STATE), F32),
        ],
        compiler_params=_cparams(("arbitrary", "arbitrary")),
        name="ssd_prompt",
    )(proj, proj, proj, x_bf, w_in_t, cw, cb, dtb, alog, dsk, nw, e_bf)


def _ssd_sample_kernel(xbc_ref, z0_ref, z1_ref, xbf_ref, wdt_ref, cs_ref, h0_ref, cw_ref, cb_ref, dtb_ref,
                       alog_ref, dsk_ref, nw_ref, e_ref, *rest):
    yn_ref, hout_ref = rest[-2:]
    t = DEC_SEQ
    r = SSD_SB * t
    l_idx = lax.broadcasted_iota(jnp.int32, (r, 1), 0) & (t - 1)
    seq_idx = lax.broadcasted_iota(jnp.int32, (r, 1), 0) // t

    def bc(v, s):
        w = v.shape[1]
        v3 = v.reshape(SSD_SB, t, w)[:, s:s + 1, :]
        return jnp.broadcast_to(v3, (SSD_SB, t, w)).reshape(r, w)

    xbc = xbc_ref[...]
    st_rows = cs_ref[...]
    acc = None
    for k in range(CONV_W):
        m = CONV_W - 1 - k
        if m == 0:
            val = xbc
        else:
            cur = pltpu.roll(xbc, m, axis=0)
            back = CONV_W - 1 - m
            stv = st_rows if back == 0 else pltpu.roll(st_rows, r - back, axis=0)
            val = jnp.where(l_idx >= m, cur, stv)
        term = val * cw_ref[k:k + 1, :]
        acc = term if acc is None else acc + term
    conv = _silu(acc + cb_ref[...])

    dt = _dt_heads(xbf_ref, wdt_ref, dtb_ref)
    a = -jnp.exp(alog_ref[...])
    acum = dt * a
    for sh in (1, 2, 4):
        acum = acum + jnp.where(l_idx >= sh, pltpu.roll(acum, sh, axis=0), 0.0)
    a_last = bc(acum, t - 1)
    e_bf = e_ref[...]
    wexp = _expand(jnp.exp(a_last - acum) * dt, e_bf)
    eexp = _expand(jnp.exp(acum), e_bf)
    cdl = jnp.exp(a_last)

    xs = conv[:, :SSD_INNER]
    bm = conv[:, SSD_INNER:SSD_INNER + SSD_GROUPS * D_STATE]
    cm = conv[:, SSD_INNER + SSD_GROUPS * D_STATE:]
    lane = lax.broadcasted_iota(jnp.int32, (r, LANES), 1)
    grp = lane // HEADS_PER_GROUP

    ms = []
    for s in range(t):
        prod = cm * bc(bm, s)
        cb = jnp.zeros((r, LANES), F32)
        for g in range(SSD_GROUPS):
            cbg = jnp.sum(prod[:, g * D_STATE:(g + 1) * D_STATE], axis=-1, keepdims=True)
            cb = jnp.where(grp == g, cbg, cb)
        dec = jnp.exp(jnp.where(l_idx >= s, acum - bc(acum, s), NEG))
        ms.append(cb * dec * bc(dt, s))
    mexp = _expand(jnp.concatenate(ms, axis=0), e_bf)
    yd = mexp[0:r, :] * bc(xs, 0)
    for s in range(1, t):
        yd = yd + mexp[s * r:(s + 1) * r, :] * bc(xs, s)

    xw = xs * wexp
    c_bf = cm.astype(BF)
    b_bf = bm.astype(BF)
    yoff_parts = []
    for g in range(SSD_GROUPS):
        gc = slice(g * GROUP_INNER, (g + 1) * GROUP_INNER)
        sc = slice(g * D_STATE, (g + 1) * D_STATE)
        hcat = h0_ref[:, gc, :].reshape(SSD_SB * GROUP_INNER, D_STATE)
        full = _dot_nt(c_bf[:, sc], hcat.astype(BF))
        yo = jnp.zeros((r, GROUP_INNER), F32)
        for qi in range(SSD_SB):
            yo = jnp.where(seq_idx == qi, full[:, qi * GROUP_INNER:(qi + 1) * GROUP_INNER], yo)
        yoff_parts.append(yo)
        for qi in range(SSD_SB):
            xq = jnp.where(seq_idx == qi, xw[:, gc], 0.0).astype(BF)
            st = _dot_tn(xq, b_bf[:, sc])
            for k in range(HEADS_PER_GROUP):
                hh = g * HEADS_PER_GROUP + k
                rows = slice(hh * SSD_HEAD_DIM, (hh + 1) * SSD_HEAD_DIM)
                cd = cdl[qi * t:qi * t + 1, hh:hh + 1]
                hout_ref[qi, rows, :] = h0_ref[qi, rows, :] * cd + st[k * SSD_HEAD_DIM:(k + 1) * SSD_HEAD_DIM, :]
    yoff = jnp.concatenate(yoff_parts, axis=1) * eexp
    y = yd + yoff + xs * dsk_ref[...]
    z = jnp.concatenate([z0_ref[...], z1_ref[...]], axis=1)
    yn_ref[...] = _gated_rmsnorm(y, z, nw_ref[...]).astype(BF)


def _ssd_sample(proj, x_bf, w_in_t, cs_pad, h0_all, cw, cb, dtb, alog, dsk, nw, e_bf, hs_prev, layer):
    r = SSD_SB * DEC_SEQ
    base = NP // r
    half = SSD_INNER // 2
    vec = lambda w: pl.BlockSpec((None, 1, w), lambda i: (layer, 0, 0))
    args = [proj, proj, proj, x_bf, w_in_t, cs_pad, h0_all, cw, cb, dtb, alog, dsk, nw, e_bf]
    in_specs = [
        pl.BlockSpec((r, CONV_DIM), lambda i: (base + i, COL_XBC // CONV_DIM)),
        pl.BlockSpec((r, half), lambda i: (base + i, COL_Z // half)),
        pl.BlockSpec((r, half), lambda i: (base + i, COL_Z // half + 1)),
        pl.BlockSpec((r, D_MODEL), lambda i: (base + i, 0)),
        pl.BlockSpec((None, LANES, D_MODEL), lambda i: (layer, W_DT // LANES, 0)),
        pl.BlockSpec((None, r, CONV_DIM), lambda i: (layer, i, 0)),
        pl.BlockSpec((None, SSD_SB, SSD_INNER, D_STATE), lambda i: (layer, i, 0, 0)),
        pl.BlockSpec((None, CONV_W, CONV_DIM), lambda i: (layer, 0, 0)),
        vec(CONV_DIM), vec(LANES), vec(LANES), vec(SSD_INNER), vec(SSD_INNER),
        pl.BlockSpec((LANES, SSD_INNER), lambda i: (0, 0)),
    ]
    aliases = {}
    if hs_prev is not None:
        aliases[len(args)] = 1
        args.append(hs_prev)
        in_specs.append(pl.BlockSpec(memory_space=pl.ANY))
    return pl.pallas_call(
        _ssd_sample_kernel,
        grid=(DEC_BATCH // SSD_SB,),
        in_specs=in_specs,
        out_specs=[
            pl.BlockSpec((r, SSD_INNER), lambda i: (i, 0)),
            pl.BlockSpec((None, SSD_SB, SSD_INNER, D_STATE), lambda i: (layer, i, 0, 0)),
        ],
        out_shape=[
            jax.ShapeDtypeStruct((NS, SSD_INNER), BF),
            jax.ShapeDtypeStruct((DEPTH, DEC_BATCH, SSD_INNER, D_STATE), F32),
        ],
        input_output_aliases=aliases,
        compiler_params=_cparams(("arbitrary",)),
        name="ssd_sample",
    )(*args)


def _mix_kernel(ap_ref, as_ref, ynp_ref, yns_ref, ga_ref, gb_ref, x_ref, wpo_ref, wso_ref, wo_ref, gab_ref,
                gbb_ref, g1_ref, b1_ref, wr_ref, rb_ref, x1_ref, rt_ref, cnt_ref, run_ref):
    is_prompt = pl.program_id(0) < NP // MIX_TM
    a = jnp.where(is_prompt, ap_ref[...], as_ref[...])
    yn = jnp.where(is_prompt, ynp_ref[...], yns_ref[...])
    pa = _dot(a, wpo_ref[...])
    ps = _dot(yn, wso_ref[...])
    m = jax.nn.sigmoid(ga_ref[...] + gab_ref[...]) * pa + jax.nn.sigmoid(gb_ref[...] + gbb_ref[...]) * ps
    res = DN_ALPHA * x_ref[...] + _dot(m.astype(BF), wo_ref[...])
    x1 = _layer_norm(res, g1_ref[...], b1_ref[...])
    x1_ref[...] = x1

    xh, xl = _split2(x1)
    wh, wl = _split2(wr_ref[...])
    logits = _dot(xh, wh) + _dot(xh, wl) + _dot(xl, wh) + rb_ref[...]
    tm = logits.shape[0]
    lane = lax.broadcasted_iota(jnp.int32, (tm, LANES), 1)
    big = 4 * LANES
    is_g = (lane >= N_EXPERTS) & (lane < N_EXPERTS + N_EXPERT_GROUPS)
    gl = jnp.where(is_g, logits, NEG)
    gmax = jnp.max(gl, axis=-1, keepdims=True)
    gidx = jnp.min(jnp.where(gl == gmax, lane, big), axis=-1, keepdims=True) - N_EXPERTS
    gsum = jnp.sum(jnp.where(is_g, jnp.exp(gl - gmax), 0.0), axis=-1, keepdims=True)
    gval = 1.0 / gsum
    in_grp = (lane < N_EXPERTS) & ((lane // EXPERTS_PER_GROUP) == gidx)
    el = jnp.where(in_grp, logits, NEG)
    m1 = jnp.max(el, axis=-1, keepdims=True)
    i1 = jnp.min(jnp.where(el == m1, lane, big), axis=-1, keepdims=True)
    el2 = jnp.where(lane == i1, NEG, el)
    m2 = jnp.max(el2, axis=-1, keepdims=True)
    i2 = jnp.min(jnp.where(el2 == m2, lane, big), axis=-1, keepdims=True)
    r21 = jnp.exp(m2 - m1)
    w1 = gval / (1.0 + r21)
    w2 = w1 * r21

    @pl.when(pl.program_id(0) == 0)
    def _():
        run_ref[...] = jnp.zeros_like(run_ref)

    oh1 = jnp.where(lane == i1, 1.0, 0.0)
    oh2 = jnp.where(lane == i2, 1.0, 0.0)
    ri = lax.broadcasted_iota(jnp.int32, (tm, tm), 0)
    ci = lax.broadcasted_iota(jnp.int32, (tm, tm), 1)
    before = jnp.where(ri > ci, 1.0, 0.0).astype(BF)
    base = run_ref[0:1, :]
    tot1 = jnp.sum(oh1, axis=0, keepdims=True)
    tot2 = jnp.sum(oh2, axis=0, keepdims=True)
    c1 = _dot(before, oh1.astype(BF)) + base
    c2 = _dot(before, oh2.astype(BF)) + (base + tot1)
    r1 = jnp.sum(jnp.where(lane == i1, c1, 0.0), axis=-1, keepdims=True)
    r2 = jnp.sum(jnp.where(lane == i2, c2, 0.0), axis=-1, keepdims=True)
    total = base + tot1 + tot2
    run_ref[0:1, :] = total
    cnt_ref[...] = jnp.broadcast_to(total, cnt_ref.shape)

    rt = jnp.zeros((tm, LANES), F32)
    for k, v in enumerate((i1.astype(F32), i2.astype(F32), w1, w2, r1, r2)):
        rt = jnp.where(lane == k, v, rt)
    rt_ref[...] = rt


def _mix(a_p, a_s, yn_p, yn_s, proj, x, wpo, wso, wo, gab, gbb, g1, b1, wr, rb, layer):
    const = lambda shape: pl.BlockSpec((None,) + shape, lambda i: (layer,) + (0,) * len(shape),
                                       pipeline_mode=pl.Buffered(1))
    vec = lambda w: pl.BlockSpec((None, 1, w), lambda i: (layer, 0, 0))
    n_p = NP // MIX_TM
    prompt_blk = lambda i: (jnp.minimum(i, n_p - 1), 0)
    sample_blk = lambda i: (jnp.maximum(i - n_p, 0), 0)
    return pl.pallas_call(
        _mix_kernel,
        grid=(NTOK // MIX_TM,),
        in_specs=[
            pl.BlockSpec((MIX_TM, POOL_DIM), prompt_blk),
            pl.BlockSpec((MIX_TM, POOL_DIM), sample_blk),
            pl.BlockSpec((MIX_TM, SSD_INNER), prompt_blk),
            pl.BlockSpec((MIX_TM, SSD_INNER), sample_blk),
            pl.BlockSpec((MIX_TM, D_MODEL), lambda i: (i, COL_GA // D_MODEL)),
            pl.BlockSpec((MIX_TM, D_MODEL), lambda i: (i, COL_GB // D_MODEL)),
            pl.BlockSpec((MIX_TM, D_MODEL), lambda i: (i, 0)),
            const((POOL_DIM, D_MODEL)), const((SSD_INNER, D_MODEL)), const((D_MODEL, D_MODEL)),
            vec(D_MODEL), vec(D_MODEL), vec(D_MODEL), vec(D_MODEL),
            const((D_MODEL, LANES)), vec(LANES),
        ],
        out_specs=[
            pl.BlockSpec((MIX_TM, D_MODEL), lambda i: (i, 0)),
            pl.BlockSpec((MIX_TM, LANES), lambda i: (i, 0)),
            pl.BlockSpec((SUBLANES, LANES), lambda i: (i, 0)),
        ],
        out_shape=[
            jax.ShapeDtypeStruct((NTOK, D_MODEL), F32),
            jax.ShapeDtypeStruct((NTOK, LANES), F32),
            jax.ShapeDtypeStruct((NTOK // MIX_TM * SUBLANES, LANES), F32),
        ],
        scratch_shapes=[pltpu.VMEM((SUBLANES, LANES), F32)],
        compiler_params=_cparams(("arbitrary",)),
        name="mix",
    )(a_p, a_s, yn_p, yn_s, proj, proj, x, wpo, wso, wo, gab, gbb, g1, b1, wr, rb)


def _route_tables(rt, cnt):
    e = jnp.concatenate([rt[:, 0], rt[:, 1]]).astype(jnp.int32)
    rank = jnp.concatenate([rt[:, 4], rt[:, 5]]).astype(jnp.int32)
    counts = cnt[-1, :N_EXPERTS].astype(jnp.int32)
    padded = ((counts + MOE_T - 1) // MOE_T) * MOE_T
    ends = jnp.cumsum(padded)
    off = ends - padded
    pos = (jnp.take(off, e) + rank).astype(jnp.int32)
    n_used = (ends[-1] // MOE_T).astype(jnp.int32)
    tiles = jnp.arange(MOE_TILES, dtype=jnp.int32)
    tile_blk = jnp.minimum(tiles, n_used - 1)
    tile_e = jnp.sum((ends[None, :] // MOE_T <= tile_blk[:, None]).astype(jnp.int32), axis=1)
    tile_e = jnp.minimum(tile_e, N_EXPERTS - 1)
    pad_lo = (ends - MOE_T).astype(jnp.int32)
    return pos, tile_blk, tile_e, n_used.reshape(1), pad_lo, counts.astype(jnp.int32)


def _row_copy(src, dst, si, di, sem):
    return pltpu.make_async_copy(src.at[pl.ds(si, 1)], dst.at[pl.ds(di, 1)], sem)


def _dispatch_kernel(pos_ref, padlo_ref, cnt_ref, x_ref, xs_hbm, zero_ref, sem):
    i = pl.program_id(0)

    @pl.when(i == 0)
    def _():
        zero_ref[...] = jnp.zeros_like(zero_ref)

        def zero_copy(e):
            lo = pl.multiple_of(padlo_ref[e], MOE_T)
            return pltpu.make_async_copy(zero_ref, xs_hbm.at[pl.ds(lo, MOE_T)], sem)

        def zstart(e, c):
            @pl.when(cnt_ref[e] > 0)
            def _():
                zero_copy(e).start()
            return c

        def zwait(e, c):
            @pl.when(cnt_ref[e] > 0)
            def _():
                zero_copy(e).wait()
            return c

        lax.fori_loop(0, N_EXPERTS, zstart, 0)
        lax.fori_loop(0, N_EXPERTS, zwait, 0)

    t0 = i * DISP_TD

    def start(j, c):
        t = t0 + j
        _row_copy(x_ref, xs_hbm, j, pos_ref[t], sem).start()
        _row_copy(x_ref, xs_hbm, j, pos_ref[NTOK + t], sem).start()
        return c

    lax.fori_loop(0, DISP_TD, start, 0, unroll=8)
    for _ in range(2):
        pltpu.make_async_copy(x_ref, xs_hbm.at[pl.ds(0, DISP_TD)], sem).wait()


def _dispatch(pos, pad_lo, counts, x1):
    return pl.pallas_call(
        _dispatch_kernel,
        grid_spec=pltpu.PrefetchScalarGridSpec(
            num_scalar_prefetch=3,
            grid=(NTOK // DISP_TD,),
            in_specs=[pl.BlockSpec((DISP_TD, D_MODEL), lambda i, pos, lo, cnt: (i, 0))],
            out_specs=pl.BlockSpec(memory_space=pl.ANY),
            scratch_shapes=[pltpu.VMEM((MOE_T, D_MODEL), F32), pltpu.SemaphoreType.DMA(())],
        ),
        out_shape=jax.ShapeDtypeStruct((MOE_ROWS, D_MODEL), F32),
        compiler_params=_cparams(("arbitrary",)),
        name="moe_dispatch",
    )(pos, pad_lo, counts, x1)


def _expert_kernel(blk_ref, te_ref, nu_ref, x_ref, wg_ref, wu_ref, wd_ref, o_ref, wg_bf, wu_bf, wd_bf):
    i = pl.program_id(0)
    prev = te_ref[jnp.maximum(i - 1, 0)]

    @pl.when((i == 0) | (te_ref[i] != prev))
    def _():
        wg_bf[...] = wg_ref[...].astype(BF)
        wu_bf[...] = wu_ref[...].astype(BF)
        wd_bf[...] = wd_ref[...].astype(BF)

    @pl.when(i < nu_ref[0])
    def _():
        x = x_ref[...].astype(BF)
        h = _silu(_dot(x, wg_bf[...])) * _dot(x, wu_bf[...])
        o_ref[...] = _dot(h.astype(BF), wd_bf[...])


def _experts(tile_blk, tile_e, n_used, xs, wg, wu, wd, layer):
    wspec = lambda a, b: pl.BlockSpec((None, None, a, b), lambda i, blk, te, nu: (layer, te[i], 0, 0))
    return pl.pallas_call(
        _expert_kernel,
        grid_spec=pltpu.PrefetchScalarGridSpec(
            num_scalar_prefetch=3,
            grid=(MOE_TILES,),
            in_specs=[
                pl.BlockSpec((MOE_T, D_MODEL), lambda i, blk, te, nu: (blk[i], 0)),
                wspec(D_MODEL, D_EXPERT), wspec(D_MODEL, D_EXPERT), wspec(D_EXPERT, D_MODEL),
            ],
            out_specs=pl.BlockSpec((MOE_T, D_MODEL), lambda i, blk, te, nu: (blk[i], 0)),
            scratch_shapes=[
                pltpu.VMEM((D_MODEL, D_EXPERT), BF),
                pltpu.VMEM((D_MODEL, D_EXPERT), BF),
                pltpu.VMEM((D_EXPERT, D_MODEL), BF),
            ],
        ),
        out_shape=jax.ShapeDtypeStruct((MOE_ROWS, D_MODEL), F32),
        compiler_params=_cparams(("arbitrary",)),
        name="moe_experts",
    )(tile_blk, tile_e, n_used, xs, wg, wu, wd)


def _combine_kernel(pos_ref, ys_hbm, x1_ref, rt_ref, g2_ref, b2_ref, x2_ref, xbf_ref, buf_ref, sem):
    i = pl.program_id(0)

    def gather(tile, par):
        def start(j, c):
            t = tile * COMB_TC + j
            _row_copy(ys_hbm, buf_ref.at[par, 0], pos_ref[t], j, sem.at[par]).start()
            _row_copy(ys_hbm, buf_ref.at[par, 1], pos_ref[NTOK + t], j, sem.at[par]).start()
            return c

        lax.fori_loop(0, COMB_TC, start, 0, unroll=8)

    @pl.when(i == 0)
    def _():
        gather(0, 0)

    @pl.when(i + 1 < pl.num_programs(0))
    def _():
        gather(i + 1, (i + 1) % 2)

    par = i % 2
    for slot in range(2):
        pltpu.make_async_copy(ys_hbm.at[pl.ds(0, COMB_TC)], buf_ref.at[par, slot], sem.at[par]).wait()
    rt = rt_ref[...]
    f = rt[:, 2:3] * buf_ref[par, 0] + rt[:, 3:4] * buf_ref[par, 1]
    x2 = _layer_norm(DN_ALPHA * x1_ref[...] + f, g2_ref[...], b2_ref[...])
    x2_ref[...] = x2
    xbf_ref[...] = x2.astype(BF)


def _combine(pos, ys, x1, rt, g2, b2, layer):
    vec = lambda w: pl.BlockSpec((None, 1, w), lambda i, pos: (layer, 0, 0))
    return pl.pallas_call(
        _combine_kernel,
        grid_spec=pltpu.PrefetchScalarGridSpec(
            num_scalar_prefetch=1,
            grid=(NTOK // COMB_TC,),
            in_specs=[
                pl.BlockSpec(memory_space=pl.ANY),
                pl.BlockSpec((COMB_TC, D_MODEL), lambda i, pos: (i, 0)),
                pl.BlockSpec((COMB_TC, LANES), lambda i, pos: (i, 0)),
                vec(D_MODEL), vec(D_MODEL),
            ],
            out_specs=[
                pl.BlockSpec((COMB_TC, D_MODEL), lambda i, pos: (i, 0)),
                pl.BlockSpec((COMB_TC, D_MODEL), lambda i, pos: (i, 0)),
            ],
            scratch_shapes=[pltpu.VMEM((2, 2, COMB_TC, D_MODEL), F32), pltpu.SemaphoreType.DMA((2,))],
        ),
        out_shape=[
            jax.ShapeDtypeStruct((NTOK, D_MODEL), F32),
            jax.ShapeDtypeStruct((NTOK, D_MODEL), BF),
        ],
        compiler_params=_cparams(("arbitrary",)),
        name="moe_combine",
    )(pos, ys, x1, rt, g2, b2)


def kernel(x_prompt, x_sample, state_ssm, state_conv, state_pool, w_in, pool_map_w, pool_map_b, pool_scale,
           conv_w, conv_b, dt_bias, a_log, d_skip, ssd_norm_w, gate_a_bias, gate_b_bias, w_pool_out, w_ssd_out,
           w_o, ln1_g, ln1_b, router_group_w, router_group_b, router_expert_w, router_expert_b, expert_w_gate,
           expert_w_up, expert_w_down, ln2_g, ln2_b):
    w_in_t = jnp.swapaxes(w_in, 1, 2)
    wpo = w_pool_out.astype(BF)
    wso = w_ssd_out.astype(BF)
    wo = w_o.astype(BF)
    r_pad = jnp.zeros((DEPTH, D_MODEL, LANES - N_EXPERTS - N_EXPERT_GROUPS), F32)
    wr = jnp.concatenate([router_expert_w, router_group_w, r_pad], axis=-1)
    rb = jnp.concatenate([router_expert_b, router_group_b,
                          jnp.zeros((DEPTH, LANES - N_EXPERTS - N_EXPERT_GROUPS), F32)], axis=-1)[:, None, :]
    row = lambda v: v[:, None, :]
    head_pad = lambda v: jnp.pad(v, ((0, 0), (0, LANES - SSD_HEADS)))[:, None, :]
    dtb = head_pad(dt_bias)
    alog = head_pad(a_log)
    dsk = row(jnp.repeat(d_skip, SSD_HEAD_DIM, axis=-1))
    e_bf = (jnp.arange(SSD_INNER)[None, :] // SSD_HEAD_DIM == jnp.arange(LANES)[:, None]).astype(BF)
    wg = expert_w_gate.reshape(DEPTH, N_EXPERTS, D_MODEL, D_EXPERT)
    wu = expert_w_up.reshape(DEPTH, N_EXPERTS, D_MODEL, D_EXPERT)
    wd = expert_w_down.reshape(DEPTH, N_EXPERTS, D_EXPERT, D_MODEL)
    cs_pad = jnp.pad(state_conv, ((0, 0), (0, 0), (0, DEC_SEQ - (CONV_W - 1)), (0, 0)))
    cs_pad = cs_pad.reshape(DEPTH, NS, CONV_DIM)
    h0_all = state_ssm.reshape(DEPTH, DEC_BATCH, SSD_INNER, D_STATE)

    x = jnp.concatenate([x_prompt.reshape(NP, D_MODEL), x_sample.reshape(NS, D_MODEL)], axis=0)
    x_bf = x.astype(BF)

    ssm_p, conv_p, pool_p, conv_s, pool_s = [], [], [], [], []
    hs_all = None
    for l in range(DEPTH):
        proj = _inproj(x_bf, w_in_t, l)
        tails = [lax.slice(proj, (b * SEQ + SEQ - POOL_BUF, 0), ((b + 1) * SEQ, COL_GA)) for b in range(BATCH)]
        tails = jnp.stack(tails, 0)
        pool_p.append(tails[:, :, COL_U:COL_U + POOL_DIM])
        conv_p.append(tails[:, POOL_BUF - (CONV_W - 1):, COL_XBC:])
        proj_s = lax.slice(proj, (NP, 0), (NTOK, COL_GA)).reshape(DEC_BATCH, DEC_SEQ, COL_GA)
        u_s = proj_s[:, :, COL_U:COL_U + POOL_DIM]
        pool_s.append(jnp.concatenate([state_pool[l][:, DEC_SEQ:], u_s], axis=1))
        conv_s.append(proj_s[:, DEC_SEQ - (CONV_W - 1):, COL_XBC:])

        pmb = row(pool_map_b.reshape(DEPTH, POOL_DIM))
        a_p = _pool_prompt(proj, pool_map_w, pmb, row(pool_scale), l)
        ext_s = jnp.concatenate([jnp.zeros((DEC_BATCH, 1, POOL_DIM), F32), state_pool[l], u_s], axis=1)
        a_s = _pool_sample(ext_s.reshape(DEC_BATCH * POOL_SLAB, POOL_DIM), pool_map_w, pmb, row(pool_scale), l)
        yn_p, h_p = _ssd_prompt(proj, x_bf, w_in_t, conv_w, row(conv_b), dtb, alog, dsk, row(ssd_norm_w),
                                e_bf, l)
        ssm_p.append(h_p)
        yn_s, hs_all = _ssd_sample(proj, x_bf, w_in_t, cs_pad, h0_all, conv_w, row(conv_b), dtb, alog, dsk,
                                   row(ssd_norm_w), e_bf, hs_all, l)
        x1, rt, cnt = _mix(a_p, a_s, yn_p, yn_s, proj, x, wpo, wso, wo, row(gate_a_bias), row(gate_b_bias),
                           row(ln1_g), row(ln1_b), wr, rb, l)
        pos, tile_blk, tile_e, n_used, pad_lo, counts = _route_tables(rt, cnt)
        xs = _dispatch(pos, pad_lo, counts, x1)
        ys = _experts(tile_blk, tile_e, n_used, xs, wg, wu, wd, l)
        x, x_bf = _combine(pos, ys, x1, rt, row(ln2_g), row(ln2_b), l)

    y_prompt = x[:NP].reshape(BATCH, SEQ, D_MODEL)
    y_sample = x[NP:].reshape(DEC_BATCH, DEC_SEQ, D_MODEL)
    new_ssm_prompt = jnp.stack(ssm_p, 0).reshape(DEPTH, BATCH, SSD_HEADS, SSD_HEAD_DIM, D_STATE)
    new_ssm_sample = hs_all.reshape(DEPTH, DEC_BATCH, SSD_HEADS, SSD_HEAD_DIM, D_STATE)
    return (y_prompt, y_sample, new_ssm_prompt, jnp.stack(conv_p, 0), jnp.stack(pool_p, 0),
            new_ssm_sample, jnp.stack(conv_s, 0), jnp.stack(pool_s, 0))
```

```python
import functools

import jax
import jax.numpy as jnp
from jax import lax
from jax.experimental import pallas as pl
from jax.experimental.pallas import tpu as pltpu

F32 = jnp.float32
BF = jnp.bfloat16

D_MODEL = 2048
BATCH = 4
SEQ = 2048
DEPTH = 4
DEC_BATCH = 128
DEC_SEQ = 8
PAST_LEN = 16384
POOL_WINDOWS = (2, 4, 8, 16)
N_POOL_GROUPS = 4
POOL_DIM = D_MODEL // 2
POOL_GROUP_DIM = POOL_DIM // N_POOL_GROUPS
POOL_BUF = max(POOL_WINDOWS) - 1
SSD_INNER = D_MODEL
SSD_HEAD_DIM = 64
SSD_HEADS = SSD_INNER // SSD_HEAD_DIM
SSD_GROUPS = 4
HEADS_PER_GROUP = SSD_HEADS // SSD_GROUPS
GROUP_INNER = SSD_INNER // SSD_GROUPS
D_STATE = 128
CONV_W = 4
CONV_DIM = SSD_INNER + 2 * SSD_GROUPS * D_STATE
CHUNK = 128
N_EXPERT_GROUPS = 4
EXPERTS_PER_GROUP = 8
N_EXPERTS = N_EXPERT_GROUPS * EXPERTS_PER_GROUP
D_EXPERT = D_MODEL // 4
DN_ALPHA = (2.0 * DEPTH) ** 0.25
LN_EPS = 1e-5
RMS_EPS = 1e-5
SPLIT_POINTS = (1024, 3072, 6144, 6176, 8224)

NP = BATCH * SEQ
NS = DEC_BATCH * DEC_SEQ
NTOK = NP + NS

LANES = 128
SUBLANES = 8
VMEM_LIMIT = 56 * 1024 * 1024

COL_U = 0
COL_Z = POOL_DIM
COL_XBC = POOL_DIM + SSD_INNER
COL_GA = POOL_DIM + SSD_INNER + CONV_DIM
COL_GB = COL_GA + D_MODEL
PROJ_W = COL_GB + D_MODEL
W_DT = SPLIT_POINTS[2]
W_GA = SPLIT_POINTS[3]

NEG = -1e30

PROJ_TM = 1024
PROJ_TN = 1024
POOL_TM = 512
POOL_SB = 16
POOL_SLAB = 24
SSD_SB = 4
MERGE_TM = 1024
MERGE_TN = 512
MIX_TM = 512
MOE_T = 256
MOE_TILES = 2 * NTOK // MOE_T + N_EXPERTS
MOE_ROWS = MOE_TILES * MOE_T
DISP_TD = 512
COMB_TC = 256


def _cparams(sem, vmem=VMEM_LIMIT):
    return pltpu.CompilerParams(dimension_semantics=sem, vmem_limit_bytes=vmem)


def _dot(a, b):
    return jnp.dot(a, b, preferred_element_type=F32)


def _dot_nt(a, b):
    return lax.dot_general(a, b, (((1,), (1,)), ((), ())), preferred_element_type=F32)


def _dot_tn(a, b):
    return lax.dot_general(a, b, (((0,), (0,)), ((), ())), preferred_element_type=F32)


def _split2(v):
    hi = v.astype(BF)
    lo = (v - hi.astype(F32)).astype(BF)
    return hi, lo


def _split3(v):
    h1 = v.astype(BF)
    r1 = v - h1.astype(F32)
    h2 = r1.astype(BF)
    h3 = (r1 - h2.astype(F32)).astype(BF)
    return h1, h2, h3


def _expand(v, e_bf):
    hi, lo = _split2(v)
    return _dot(hi, e_bf) + _dot(lo, e_bf)


def _silu(x):
    return x * jax.nn.sigmoid(x)


def _softplus(x):
    return jnp.maximum(x, 0.0) + jnp.log1p(jnp.exp(-jnp.abs(x)))


def _layer_norm(x, g, b):
    mu = jnp.mean(x, axis=-1, keepdims=True)
    xc = x - mu
    var = jnp.mean(xc * xc, axis=-1, keepdims=True)
    return xc * lax.rsqrt(var + LN_EPS) * g + b


def _gated_rmsnorm(y, z, nw):
    v = y * _silu(z)
    outs = []
    for g in range(SSD_GROUPS):
        vg = v[:, g * GROUP_INNER:(g + 1) * GROUP_INNER]
        ms = jnp.mean(vg * vg, axis=-1, keepdims=True)
        outs.append(vg * lax.rsqrt(ms + RMS_EPS))
    return jnp.concatenate(outs, axis=-1) * nw


def _inproj_kernel(x_ref, wt_ref, o_ref, wbf_ref):
    @pl.when(pl.program_id(1) == 0)
    def _():
        wbf_ref[...] = wt_ref[0].astype(BF)

    o_ref[...] = _dot_nt(x_ref[...], wbf_ref[...])


def _inproj(x_bf, w_in_t, layer):
    n_main = COL_GA // PROJ_TN

    def w_rows(j, i):
        row0 = j * PROJ_TN + jnp.where(j >= n_main, W_GA - W_DT, 0)
        return (layer, pl.multiple_of(row0, W_GA - W_DT), 0)

    return pl.pallas_call(
        _inproj_kernel,
        grid=(PROJ_W // PROJ_TN, NTOK // PROJ_TM),
        in_specs=[
            pl.BlockSpec((PROJ_TM, D_MODEL), lambda j, i: (i, 0)),
            pl.BlockSpec((pl.Element(1), pl.Element(PROJ_TN), pl.Element(D_MODEL)), w_rows),
        ],
        out_specs=pl.BlockSpec((PROJ_TM, PROJ_TN), lambda j, i: (i, j)),
        out_shape=jax.ShapeDtypeStruct((NTOK, PROJ_W), F32),
        scratch_shapes=[pltpu.VMEM((PROJ_TN, D_MODEL), BF)],
        compiler_params=_cparams(("arbitrary", "arbitrary")),
        name="inproj",
    )(x_bf, w_in_t)


def _pool_group(eg, lead, u_rows, pos, win, mw, mb, sc, take):
    s = eg
    sh = 1
    while sh < win:
        s = s + pltpu.roll(s, sh, axis=0)
        sh *= 2
    s = take(s)
    cnt = jnp.minimum(pos + 1, win).astype(F32)
    d = s / cnt - u_rows
    mixed = _dot(d.astype(BF), mw.astype(BF)) + mb
    return mixed * sc


def _pool_prompt_kernel(u_ref, mw_ref, mb_ref, sc_ref, o_ref, ext_ref):
    r = pl.program_id(1)
    halo = 2 * SUBLANES

    @pl.when(r == 0)
    def _():
        ext_ref[0:halo, :] = jnp.zeros((halo, POOL_DIM), F32)

    u = u_ref[...]
    ext_ref[halo:halo + POOL_TM, :] = u
    e = ext_ref[...]
    pos = r * POOL_TM + lax.broadcasted_iota(jnp.int32, (POOL_TM, 1), 0)
    for g, win in enumerate(POOL_WINDOWS):
        cs = slice(g * POOL_GROUP_DIM, (g + 1) * POOL_GROUP_DIM)
        out = _pool_group(e[:, cs], halo, u[:, cs], pos, win, mw_ref[g], mb_ref[:, cs], sc_ref[:, cs],
                          lambda s: s[halo:, :])
        o_ref[:, cs] = out.astype(BF)
    ext_ref[0:halo, :] = u[POOL_TM - halo:, :]


def _pool_prompt(proj, mw, mb, sc, layer):
    rt = SEQ // POOL_TM
    return pl.pallas_call(
        _pool_prompt_kernel,
        grid=(BATCH, rt),
        in_specs=[
            pl.BlockSpec((POOL_TM, POOL_DIM), lambda b, r: (b * rt + r, COL_U // POOL_DIM)),
            pl.BlockSpec((None, N_POOL_GROUPS, POOL_GROUP_DIM, POOL_GROUP_DIM), lambda b, r: (layer, 0, 0, 0)),
            pl.BlockSpec((None, 1, POOL_DIM), lambda b, r: (layer, 0, 0)),
            pl.BlockSpec((None, 1, POOL_DIM), lambda b, r: (layer, 0, 0)),
        ],
        out_specs=pl.BlockSpec((POOL_TM, POOL_DIM), lambda b, r: (b * rt + r, 0)),
        out_shape=jax.ShapeDtypeStruct((NP, POOL_DIM), BF),
        scratch_shapes=[pltpu.VMEM((POOL_TM + 2 * SUBLANES, POOL_DIM), F32)],
        compiler_params=_cparams(("arbitrary", "arbitrary")),
        name="pool_prompt",
    )(proj, mw, mb, sc)


def _pool_sample_kernel(ext_ref, mw_ref, mb_ref, sc_ref, o_ref):
    e = ext_ref[...]
    rows = POOL_SB * DEC_SEQ
    first = POOL_SLAB - DEC_SEQ
    pos = PAST_LEN + (lax.broadcasted_iota(jnp.int32, (rows, 1), 0) & (DEC_SEQ - 1))

    def take(s):
        s3 = s.reshape(POOL_SB, POOL_SLAB, POOL_GROUP_DIM)[:, first:, :]
        return s3.reshape(rows, POOL_GROUP_DIM)

    for g, win in enumerate(POOL_WINDOWS):
        cs = slice(g * POOL_GROUP_DIM, (g + 1) * POOL_GROUP_DIM)
        eg = e[:, cs]
        out = _pool_group(eg, first, take(eg), pos, win, mw_ref[g], mb_ref[:, cs], sc_ref[:, cs], take)
        o_ref[:, cs] = out.astype(BF)


def _pool_sample(ext_s, mw, mb, sc, layer):
    rows = POOL_SB * DEC_SEQ
    return pl.pallas_call(
        _pool_sample_kernel,
        grid=(DEC_BATCH // POOL_SB,),
        in_specs=[
            pl.BlockSpec((POOL_SB * POOL_SLAB, POOL_DIM), lambda i: (i, 0)),
            pl.BlockSpec((None, N_POOL_GROUPS, POOL_GROUP_DIM, POOL_GROUP_DIM), lambda i: (layer, 0, 0, 0)),
            pl.BlockSpec((None, 1, POOL_DIM), lambda i: (layer, 0, 0)),
            pl.BlockSpec((None, 1, POOL_DIM), lambda i: (layer, 0, 0)),
        ],
        out_specs=pl.BlockSpec((rows, POOL_DIM), lambda i: (i, 0)),
        out_shape=jax.ShapeDtypeStruct((NS, POOL_DIM), BF),
        compiler_params=_cparams(("arbitrary",)),
        name="pool_sample",
    )(ext_s, mw, mb, sc)


def _dt_heads(xbf_ref, wdt_ref, dtb_ref):
    raw = _dot_nt(xbf_ref[...], wdt_ref[...].astype(BF))
    lane = lax.broadcasted_iota(jnp.int32, raw.shape, 1)
    return _softplus(jnp.where(lane < SSD_HEADS, raw, 0.0) + dtb_ref[...])


def _ssd_prompt_kernel(xbc_ref, z0_ref, z1_ref, xbf_ref, wdt_ref, cw_ref, cb_ref, dtb_ref, alog_ref, dsk_ref,
                       nw_ref, e_ref, yn_ref, hout_ref, ext_ref, h_ref):
    c = pl.program_id(1)
    q = CHUNK
    halo = SUBLANES

    @pl.when(c == 0)
    def _():
        ext_ref[0:halo, :] = jnp.zeros((halo, CONV_DIM), F32)
        h_ref[...] = jnp.zeros_like(h_ref)

    xbc = xbc_ref[...]
    ext_ref[halo:halo + q, :] = xbc
    first = halo - (CONV_W - 1)
    acc = ext_ref[first:first + q, :] * cw_ref[0:1, :]
    for k in range(1, CONV_W):
        acc = acc + ext_ref[first + k:first + k + q, :] * cw_ref[k:k + 1, :]
    ext_ref[0:halo, :] = xbc[q - halo:, :]
    conv = _silu(acc + cb_ref[...])

    dt = _dt_heads(xbf_ref, wdt_ref, dtb_ref)
    a = -jnp.exp(alog_ref[...])
    da = dt * a
    ri = lax.broadcasted_iota(jnp.int32, (q, q), 0)
    ci = lax.broadcasted_iota(jnp.int32, (q, q), 1)
    causal = ri >= ci
    tril = jnp.where(causal, 1.0, 0.0).astype(BF)
    d1, d2, d3 = _split3(da)
    acum = _dot(tril, d1) + _dot(tril, d2) + _dot(tril, d3)
    acum_t = acum.T
    dt_t = dt.T
    a_last = acum[q - 1:q, :]
    e_bf = e_ref[...]
    wexp = _expand(jnp.exp(a_last - acum) * dt, e_bf)
    eexp = _expand(jnp.exp(acum), e_bf)
    lane = lax.broadcasted_iota(jnp.int32, (q, LANES), 1)
    low = lane < SSD_HEAD_DIM

    y_parts = []
    for g in range(SSD_GROUPS):
        gc = slice(g * GROUP_INNER, (g + 1) * GROUP_INNER)
        b_g = conv[:, SSD_INNER + g * D_STATE:SSD_INNER + (g + 1) * D_STATE].astype(BF)
        c0 = SSD_INNER + SSD_GROUPS * D_STATE
        c_g = conv[:, c0 + g * D_STATE:c0 + (g + 1) * D_STATE].astype(BF)
        cb = _dot_nt(c_g, b_g)
        xg = conv[:, gc]
        xg_bf = xg.astype(BF)
        yd = []
        for pr in range(HEADS_PER_GROUP // 2):
            ms = []
            for j in range(2):
                hh = g * HEADS_PER_GROUP + pr * 2 + j
                seg = acum[:, hh:hh + 1] - acum_t[hh:hh + 1, :]
                dec = jnp.exp(jnp.where(causal, seg, NEG))
                ms.append((cb * dec * dt_t[hh:hh + 1, :]).astype(BF))
            mp = jnp.concatenate(ms, axis=1)
            xp = xg_bf[:, pr * LANES:(pr + 1) * LANES]
            zero = jnp.zeros_like(xp)
            xbd = jnp.concatenate([jnp.where(low, xp, zero), jnp.where(low, zero, xp)], axis=0)
            yd.append(_dot(mp, xbd))
        yd = jnp.concatenate(yd, axis=1)
        hg = h_ref[g * GROUP_INNER:(g + 1) * GROUP_INNER, :]
        yoff = _dot_nt(c_g, hg.astype(BF)) * eexp[:, gc]
        y_parts.append(yd + yoff + xg * dsk_ref[:, gc])
        xw = (xg * wexp[:, gc]).astype(BF)
        st = _dot_tn(xw, b_g)
        for k in range(HEADS_PER_GROUP):
            hh = g * HEADS_PER_GROUP + k
            rows = slice(hh * SSD_HEAD_DIM, (hh + 1) * SSD_HEAD_DIM)
            cd = jnp.exp(a_last[:, hh:hh + 1])
            h_ref[rows, :] = h_ref[rows, :] * cd + st[k * SSD_HEAD_DIM:(k + 1) * SSD_HEAD_DIM, :]
    y = jnp.concatenate(y_parts, axis=1)
    z = jnp.concatenate([z0_ref[...], z1_ref[...]], axis=1)
    yn_ref[...] = _gated_rmsnorm(y, z, nw_ref[...]).astype(BF)

    @pl.when(c == pl.num_programs(1) - 1)
    def _():
        hout_ref[...] = h_ref[...]


def _ssd_prompt(proj, x_bf, w_in_t, cw, cb, dtb, alog, dsk, nw, e_bf, layer):
    nc = SEQ // CHUNK
    vec = lambda w: pl.BlockSpec((None, 1, w), lambda b, c: (layer, 0, 0))
    half = SSD_INNER // 2
    return pl.pallas_call(
        _ssd_prompt_kernel,
        grid=(BATCH, nc),
        in_specs=[
            pl.BlockSpec((CHUNK, CONV_DIM), lambda b, c: (b * nc + c, COL_XBC // CONV_DIM)),
            pl.BlockSpec((CHUNK, half), lambda b, c: (b * nc + c, COL_Z // half)),
            pl.BlockSpec((CHUNK, half), lambda b, c: (b * nc + c, COL_Z // half + 1)),
            pl.BlockSpec((CHUNK, D_MODEL), lambda b, c: (b * nc + c, 0)),
            pl.BlockSpec((None, LANES, D_MODEL), lambda b, c: (layer, W_DT // LANES, 0)),
            pl.BlockSpec((None, CONV_W, CONV_DIM), lambda b, c: (layer, 0, 0)),
            vec(CONV_DIM), vec(LANES), vec(LANES), vec(SSD_INNER), vec(SSD_INNER),
            pl.BlockSpec((LANES, SSD_INNER), lambda b, c: (0, 0)),
        ],
        out_specs=[
            pl.BlockSpec((CHUNK, SSD_INNER), lambda b, c: (b * nc + c, 0)),
            pl.BlockSpec((None, SSD_INNER, D_STATE), lambda b, c: (b, 0, 0)),
        ],
        out_shape=[
            jax.ShapeDtypeStruct((NP, SSD_INNER), BF),
            jax.ShapeDtypeStruct((BATCH, SSD_INNER, D_STATE), F32),
        ],
        scratch_shapes=[
            pltpu.VMEM((CHUNK + SUBLANES, CONV_DIM), F32),
            pltpu.VMEM((SSD_INNER, D_STATE), F32),
        ],
        compiler_params=_cparams(("arbitrary", "arbitrary")),
        name="ssd_prompt",
    )(proj, proj, proj, x_bf, w_in_t, cw, cb, dtb, alog, dsk, nw, e_bf)


def _ssd_sample_kernel(xbc_ref, z0_ref, z1_ref, xbf_ref, wdt_ref, cs_ref, h0_ref, cw_ref, cb_ref, dtb_ref,
                       alog_ref, dsk_ref, nw_ref, e_ref, *rest):
    yn_ref, hout_ref = rest[-2:]
    t = DEC_SEQ
    r = SSD_SB * t
    l_idx = lax.broadcasted_iota(jnp.int32, (r, 1), 0) & (t - 1)
    seq_idx = lax.broadcasted_iota(jnp.int32, (r, 1), 0) // t

    def bc(v, s):
        w = v.shape[1]
        v3 = v.reshape(SSD_SB, t, w)[:, s:s + 1, :]
        return jnp.broadcast_to(v3, (SSD_SB, t, w)).reshape(r, w)

    xbc = xbc_ref[...]
    st_rows = cs_ref[...]
    acc = None
    for k in range(CONV_W):
        m = CONV_W - 1 - k
        if m == 0:
            val = xbc
        else:
            cur = pltpu.roll(xbc, m, axis=0)
            back = CONV_W - 1 - m
            stv = st_rows if back == 0 else pltpu.roll(st_rows, r - back, axis=0)
            val = jnp.where(l_idx >= m, cur, stv)
        term = val * cw_ref[k:k + 1, :]
        acc = term if acc is None else acc + term
    conv = _silu(acc + cb_ref[...])

    dt = _dt_heads(xbf_ref, wdt_ref, dtb_ref)
    a = -jnp.exp(alog_ref[...])
    acum = dt * a
    for sh in (1, 2, 4):
        acum = acum + jnp.where(l_idx >= sh, pltpu.roll(acum, sh, axis=0), 0.0)
    a_last = bc(acum, t - 1)
    e_bf = e_ref[...]
    wexp = _expand(jnp.exp(a_last - acum) * dt, e_bf)
    eexp = _expand(jnp.exp(acum), e_bf)
    cdl = jnp.exp(a_last)

    xs = conv[:, :SSD_INNER]
    bm = conv[:, SSD_INNER:SSD_INNER + SSD_GROUPS * D_STATE]
    cm = conv[:, SSD_INNER + SSD_GROUPS * D_STATE:]
    lane = lax.broadcasted_iota(jnp.int32, (r, LANES), 1)
    grp = lane // HEADS_PER_GROUP

    ms = []
    for s in range(t):
        prod = cm * bc(bm, s)
        cb = jnp.zeros((r, LANES), F32)
        for g in range(SSD_GROUPS):
            cbg = jnp.sum(prod[:, g * D_STATE:(g + 1) * D_STATE], axis=-1, keepdims=True)
            cb = jnp.where(grp == g, cbg, cb)
        dec = jnp.exp(jnp.where(l_idx >= s, acum - bc(acum, s), NEG))
        ms.append(cb * dec * bc(dt, s))
    mexp = _expand(jnp.concatenate(ms, axis=0), e_bf)
    yd = mexp[0:r, :] * bc(xs, 0)
    for s in range(1, t):
        yd = yd + mexp[s * r:(s + 1) * r, :] * bc(xs, s)

    xw = xs * wexp
    c_bf = cm.astype(BF)
    b_bf = bm.astype(BF)
    yoff_parts = []
    for g in range(SSD_GROUPS):
        gc = slice(g * GROUP_INNER, (g + 1) * GROUP_INNER)
        sc = slice(g * D_STATE, (g + 1) * D_STATE)
        hcat = h0_ref[:, gc, :].reshape(SSD_SB * GROUP_INNER, D_STATE)
        full = _dot_nt(c_bf[:, sc], hcat.astype(BF))
        yo = jnp.zeros((r, GROUP_INNER), F32)
        for qi in range(SSD_SB):
            yo = jnp.where(seq_idx == qi, full[:, qi * GROUP_INNER:(qi + 1) * GROUP_INNER], yo)
        yoff_parts.append(yo)
        for qi in range(SSD_SB):
            xq = jnp.where(seq_idx == qi, xw[:, gc], 0.0).astype(BF)
            st = _dot_tn(xq, b_bf[:, sc])
            for k in range(HEADS_PER_GROUP):
                hh = g * HEADS_PER_GROUP + k
                rows = slice(hh * SSD_HEAD_DIM, (hh + 1) * SSD_HEAD_DIM)
                cd = cdl[qi * t:qi * t + 1, hh:hh + 1]
                hout_ref[qi, rows, :] = h0_ref[qi, rows, :] * cd + st[k * SSD_HEAD_DIM:(k + 1) * SSD_HEAD_DIM, :]
    yoff = jnp.concatenate(yoff_parts, axis=1) * eexp
    y = yd + yoff + xs * dsk_ref[...]
    z = jnp.concatenate([z0_ref[...], z1_ref[...]], axis=1)
    yn_ref[...] = _gated_rmsnorm(y, z, nw_ref[...]).astype(BF)


def _ssd_sample(proj, x_bf, w_in_t, cs_pad, h0_all, cw, cb, dtb, alog, dsk, nw, e_bf, hs_prev, layer):
    r = SSD_SB * DEC_SEQ
    base = NP // r
    half = SSD_INNER // 2
    vec = lambda w: pl.BlockSpec((None, 1, w), lambda i: (layer, 0, 0))
    args = [proj, proj, proj, x_bf, w_in_t, cs_pad, h0_all, cw, cb, dtb, alog, dsk, nw, e_bf]
    in_specs = [
        pl.BlockSpec((r, CONV_DIM), lambda i: (base + i, COL_XBC // CONV_DIM)),
        pl.BlockSpec((r, half), lambda i: (base + i, COL_Z // half)),
        pl.BlockSpec((r, half), lambda i: (base + i, COL_Z // half + 1)),
        pl.BlockSpec((r, D_MODEL), lambda i: (base + i, 0)),
        pl.BlockSpec((None, LANES, D_MODEL), lambda i: (layer, W_DT // LANES, 0)),
        pl.BlockSpec((None, r, CONV_DIM), lambda i: (layer, i, 0)),
        pl.BlockSpec((None, SSD_SB, SSD_INNER, D_STATE), lambda i: (layer, i, 0, 0)),
        pl.BlockSpec((None, CONV_W, CONV_DIM), lambda i: (layer, 0, 0)),
        vec(CONV_DIM), vec(LANES), vec(LANES), vec(SSD_INNER), vec(SSD_INNER),
        pl.BlockSpec((LANES, SSD_INNER), lambda i: (0, 0)),
    ]
    aliases = {}
    if hs_prev is not None:
        aliases[len(args)] = 1
        args.append(hs_prev)
        in_specs.append(pl.BlockSpec(memory_space=pl.ANY))
    return pl.pallas_call(
        _ssd_sample_kernel,
        grid=(DEC_BATCH // SSD_SB,),
        in_specs=in_specs,
        out_specs=[
            pl.BlockSpec((r, SSD_INNER), lambda i: (i, 0)),
            pl.BlockSpec((None, SSD_SB, SSD_INNER, D_STATE), lambda i: (layer, i, 0, 0)),
        ],
        out_shape=[
            jax.ShapeDtypeStruct((NS, SSD_INNER), BF),
            jax.ShapeDtypeStruct((DEPTH, DEC_BATCH, SSD_INNER, D_STATE), F32),
        ],
        input_output_aliases=aliases,
        compiler_params=_cparams(("arbitrary",)),
        name="ssd_sample",
    )(*args)


def _merge_kernel(ap_ref, as_ref, ynp_ref, yns_ref, ga_ref, gb_ref, wpo_ref, wso_ref, gab_ref, gbb_ref, m_ref,
                  wpo_bf, wso_bf):
    i = pl.program_id(1)

    @pl.when(i == 0)
    def _():
        wpo_bf[...] = wpo_ref[...].astype(BF)
        wso_bf[...] = wso_ref[...].astype(BF)

    is_prompt = i < NP // MERGE_TM
    a = jnp.where(is_prompt, ap_ref[...], as_ref[...])
    yn = jnp.where(is_prompt, ynp_ref[...], yns_ref[...])
    pa = _dot(a, wpo_bf[...])
    ps = _dot(yn, wso_bf[...])
    m = jax.nn.sigmoid(ga_ref[...] + gab_ref[...]) * pa + jax.nn.sigmoid(gb_ref[...] + gbb_ref[...]) * ps
    m_ref[...] = m.astype(BF)


def _merge(a_p, a_s, yn_p, yn_s, proj, wpo, wso, gab, gbb, layer):
    n_p = NP // MERGE_TM
    prompt_blk = lambda j, i: (jnp.minimum(i, n_p - 1), 0)
    sample_blk = lambda j, i: (jnp.maximum(i - n_p, 0), 0)
    vec = pl.BlockSpec((None, 1, MERGE_TN), lambda j, i: (layer, 0, j))
    return pl.pallas_call(
        _merge_kernel,
        grid=(D_MODEL // MERGE_TN, NTOK // MERGE_TM),
        in_specs=[
            pl.BlockSpec((MERGE_TM, POOL_DIM), prompt_blk),
            pl.BlockSpec((MERGE_TM, POOL_DIM), sample_blk),
            pl.BlockSpec((MERGE_TM, SSD_INNER), prompt_blk),
            pl.BlockSpec((MERGE_TM, SSD_INNER), sample_blk),
            pl.BlockSpec((MERGE_TM, MERGE_TN), lambda j, i: (i, COL_GA // MERGE_TN + j)),
            pl.BlockSpec((MERGE_TM, MERGE_TN), lambda j, i: (i, COL_GB // MERGE_TN + j)),
            pl.BlockSpec((None, POOL_DIM, MERGE_TN), lambda j, i: (layer, 0, j)),
            pl.BlockSpec((None, SSD_INNER, MERGE_TN), lambda j, i: (layer, 0, j)),
            vec, vec,
        ],
        out_specs=pl.BlockSpec((MERGE_TM, MERGE_TN), lambda j, i: (i, j)),
        out_shape=jax.ShapeDtypeStruct((NTOK, D_MODEL), BF),
        scratch_shapes=[pltpu.VMEM((POOL_DIM, MERGE_TN), BF), pltpu.VMEM((SSD_INNER, MERGE_TN), BF)],
        compiler_params=_cparams(("arbitrary", "arbitrary")),
        name="merge",
    )(a_p, a_s, yn_p, yn_s, proj, proj, wpo, wso, gab, gbb)


def _mix_kernel(m_ref, x_ref, wo_ref, g1_ref, b1_ref, wr_ref, rb_ref, x1_ref, rt_ref, cnt_ref, wo_bf, run_ref):
    @pl.when(pl.program_id(0) == 0)
    def _():
        wo_bf[...] = wo_ref[...].astype(BF)

    res = DN_ALPHA * x_ref[...] + _dot(m_ref[...], wo_bf[...])
    x1 = _layer_norm(res, g1_ref[...], b1_ref[...])
    x1_ref[...] = x1

    xh, xl = _split2(x1)
    wh, wl = _split2(wr_ref[...])
    logits = _dot(xh, wh) + _dot(xh, wl) + _dot(xl, wh) + rb_ref[...]
    tm = logits.shape[0]
    lane = lax.broadcasted_iota(jnp.int32, (tm, LANES), 1)
    big = 4 * LANES
    is_g = (lane >= N_EXPERTS) & (lane < N_EXPERTS + N_EXPERT_GROUPS)
    gl = jnp.where(is_g, logits, NEG)
    gmax = jnp.max(gl, axis=-1, keepdims=True)
    gidx = jnp.min(jnp.where(gl == gmax, lane, big), axis=-1, keepdims=True) - N_EXPERTS
    gsum = jnp.sum(jnp.where(is_g, jnp.exp(gl - gmax), 0.0), axis=-1, keepdims=True)
    gval = 1.0 / gsum
    in_grp = (lane < N_EXPERTS) & ((lane // EXPERTS_PER_GROUP) == gidx)
    el = jnp.where(in_grp, logits, NEG)
    m1 = jnp.max(el, axis=-1, keepdims=True)
    i1 = jnp.min(jnp.where(el == m1, lane, big), axis=-1, keepdims=True)
    el2 = jnp.where(lane == i1, NEG, el)
    m2 = jnp.max(el2, axis=-1, keepdims=True)
    i2 = jnp.min(jnp.where(el2 == m2, lane, big), axis=-1, keepdims=True)
    r21 = jnp.exp(m2 - m1)
    w1 = gval / (1.0 + r21)
    w2 = w1 * r21

    @pl.when(pl.program_id(0) == 0)
    def _():
        run_ref[...] = jnp.zeros_like(run_ref)

    oh1 = jnp.where(lane == i1, 1.0, 0.0)
    oh2 = jnp.where(lane == i2, 1.0, 0.0)
    ri = lax.broadcasted_iota(jnp.int32, (tm, tm), 0)
    ci = lax.broadcasted_iota(jnp.int32, (tm, tm), 1)
    before = jnp.where(ri > ci, 1.0, 0.0).astype(BF)
    base = run_ref[0:1, :]
    tot1 = jnp.sum(oh1, axis=0, keepdims=True)
    tot2 = jnp.sum(oh2, axis=0, keepdims=True)
    c1 = _dot(before, oh1.astype(BF)) + base
    c2 = _dot(before, oh2.astype(BF)) + (base + tot1)
    r1 = jnp.sum(jnp.where(lane == i1, c1, 0.0), axis=-1, keepdims=True)
    r2 = jnp.sum(jnp.where(lane == i2, c2, 0.0), axis=-1, keepdims=True)
    total = base + tot1 + tot2
    run_ref[0:1, :] = total
    cnt_ref[...] = jnp.broadcast_to(total, cnt_ref.shape)

    rt = jnp.zeros((tm, LANES), F32)
    for k, v in enumerate((i1.astype(F32), i2.astype(F32), w1, w2, r1, r2)):
        rt = jnp.where(lane == k, v, rt)
    rt_ref[...] = rt


def _mix(m, x, wo, g1, b1, wr, rb, layer):
    const = lambda shape: pl.BlockSpec((None,) + shape, lambda i: (layer,) + (0,) * len(shape),
                                       pipeline_mode=pl.Buffered(1))
    vec = lambda w: pl.BlockSpec((None, 1, w), lambda i: (layer, 0, 0))
    return pl.pallas_call(
        _mix_kernel,
        grid=(NTOK // MIX_TM,),
        in_specs=[
            pl.BlockSpec((MIX_TM, D_MODEL), lambda i: (i, 0)),
            pl.BlockSpec((MIX_TM, D_MODEL), lambda i: (i, 0)),
            const((D_MODEL, D_MODEL)),
            vec(D_MODEL), vec(D_MODEL),
            const((D_MODEL, LANES)), vec(LANES),
        ],
        out_specs=[
            pl.BlockSpec((MIX_TM, D_MODEL), lambda i: (i, 0)),
            pl.BlockSpec((MIX_TM, LANES), lambda i: (i, 0)),
            pl.BlockSpec((SUBLANES, LANES), lambda i: (i, 0)),
        ],
        out_shape=[
            jax.ShapeDtypeStruct((NTOK, D_MODEL), F32),
            jax.ShapeDtypeStruct((NTOK, LANES), F32),
            jax.ShapeDtypeStruct((NTOK // MIX_TM * SUBLANES, LANES), F32),
        ],
        scratch_shapes=[pltpu.VMEM((D_MODEL, D_MODEL), BF), pltpu.VMEM((SUBLANES, LANES), F32)],
        compiler_params=_cparams(("arbitrary",)),
        name="mix",
    )(m, x, wo, g1, b1, wr, rb)


def _route_tables(rt, cnt):
    e = jnp.concatenate([rt[:, 0], rt[:, 1]]).astype(jnp.int32)
    rank = jnp.concatenate([rt[:, 4], rt[:, 5]]).astype(jnp.int32)
    counts = cnt[-1, :N_EXPERTS].astype(jnp.int32)
    padded = ((counts + MOE_T - 1) // MOE_T) * MOE_T
    ends = jnp.cumsum(padded)
    off = ends - padded
    pos = (jnp.take(off, e) + rank).astype(jnp.int32)
    n_used = (ends[-1] // MOE_T).astype(jnp.int32)
    tiles = jnp.arange(MOE_TILES, dtype=jnp.int32)
    tile_blk = jnp.minimum(tiles, n_used - 1)
    tile_e = jnp.sum((ends[None, :] // MOE_T <= tile_blk[:, None]).astype(jnp.int32), axis=1)
    tile_e = jnp.minimum(tile_e, N_EXPERTS - 1)
    later = (tile_e[None, :] > tile_e[:, None]) & (tiles[None, :] < n_used)
    tile_next = jnp.min(jnp.where(later, tile_e[None, :], N_EXPERTS), axis=1)
    tile_next = jnp.where(tile_next == N_EXPERTS, -1, tile_next).astype(jnp.int32)
    pad_lo = (ends - MOE_T).astype(jnp.int32)
    return pos, tile_blk, tile_e, tile_next, n_used.reshape(1), pad_lo, counts.astype(jnp.int32)


def _row_copy(src, dst, si, di, sem):
    return pltpu.make_async_copy(src.at[pl.ds(si, 1)], dst.at[pl.ds(di, 1)], sem)


def _dispatch_kernel(pos_ref, padlo_ref, cnt_ref, x_ref, xs_hbm, zero_ref, sem):
    i = pl.program_id(0)

    @pl.when(i == 0)
    def _():
        zero_ref[...] = jnp.zeros_like(zero_ref)

        def zero_copy(e):
            lo = pl.multiple_of(padlo_ref[e], MOE_T)
            return pltpu.make_async_copy(zero_ref, xs_hbm.at[pl.ds(lo, MOE_T)], sem)

        def zstart(e, c):
            @pl.when(cnt_ref[e] > 0)
            def _():
                zero_copy(e).start()
            return c

        def zwait(e, c):
            @pl.when(cnt_ref[e] > 0)
            def _():
                zero_copy(e).wait()
            return c

        lax.fori_loop(0, N_EXPERTS, zstart, 0)
        lax.fori_loop(0, N_EXPERTS, zwait, 0)

    t0 = i * DISP_TD

    def start(j, c):
        t = t0 + j
        _row_copy(x_ref, xs_hbm, j, pos_ref[t], sem).start()
        _row_copy(x_ref, xs_hbm, j, pos_ref[NTOK + t], sem).start()
        return c

    lax.fori_loop(0, DISP_TD, start, 0, unroll=8)
    for _ in range(2):
        pltpu.make_async_copy(x_ref, xs_hbm.at[pl.ds(0, DISP_TD)], sem).wait()


def _dispatch(pos, pad_lo, counts, x1):
    return pl.pallas_call(
        _dispatch_kernel,
        grid_spec=pltpu.PrefetchScalarGridSpec(
            num_scalar_prefetch=3,
            grid=(NTOK // DISP_TD,),
            in_specs=[pl.BlockSpec((DISP_TD, D_MODEL), lambda i, pos, lo, cnt: (i, 0))],
            out_specs=pl.BlockSpec(memory_space=pl.ANY),
            scratch_shapes=[pltpu.VMEM((MOE_T, D_MODEL), F32), pltpu.SemaphoreType.DMA(())],
        ),
        out_shape=jax.ShapeDtypeStruct((MOE_ROWS, D_MODEL), F32),
        compiler_params=_cparams(("arbitrary",)),
        name="moe_dispatch",
    )(pos, pad_lo, counts, x1)


def _expert_kernel(blk_ref, te_ref, nxt_ref, nu_ref, x_ref, wg_hbm, wu_hbm, wd_hbm, o_ref,
                   wg_st, wu_st, wd_st, wg_bf, wu_bf, wd_bf, sem, *, layer):
    i = pl.program_id(0)
    e = te_ref[i]
    first = (i == 0) | (e != te_ref[jnp.maximum(i - 1, 0)])

    def fetch(ex):
        return (pltpu.make_async_copy(wg_hbm.at[layer, ex], wg_st, sem.at[0]),
                pltpu.make_async_copy(wu_hbm.at[layer, ex], wu_st, sem.at[1]),
                pltpu.make_async_copy(wd_hbm.at[layer, ex], wd_st, sem.at[2]))

    @pl.when(i == 0)
    def _():
        for c in fetch(e):
            c.start()

    @pl.when(first)
    def _():
        for c in fetch(e):
            c.wait()
        wg_bf[...] = wg_st[...].astype(BF)
        wu_bf[...] = wu_st[...].astype(BF)
        wd_bf[...] = wd_st[...].astype(BF)
        nxt = nxt_ref[i]

        @pl.when(nxt >= 0)
        def _():
            for c in fetch(nxt):
                c.start()

    @pl.when(i < nu_ref[0])
    def _():
        x = x_ref[...].astype(BF)
        h = _silu(_dot(x, wg_bf[...])) * _dot(x, wu_bf[...])
        o_ref[...] = _dot(h.astype(BF), wd_bf[...])


def _experts(tile_blk, tile_e, tile_next, n_used, xs, wg, wu, wd, layer):
    hbm = pl.BlockSpec(memory_space=pl.ANY)
    return pl.pallas_call(
        functools.partial(_expert_kernel, layer=layer),
        grid_spec=pltpu.PrefetchScalarGridSpec(
            num_scalar_prefetch=4,
            grid=(MOE_TILES,),
            in_specs=[pl.BlockSpec((MOE_T, D_MODEL), lambda i, blk, te, nxt, nu: (blk[i], 0)), hbm, hbm, hbm],
            out_specs=pl.BlockSpec((MOE_T, D_MODEL), lambda i, blk, te, nxt, nu: (blk[i], 0)),
            scratch_shapes=[
                pltpu.VMEM((D_MODEL, D_EXPERT), F32),
                pltpu.VMEM((D_MODEL, D_EXPERT), F32),
                pltpu.VMEM((D_EXPERT, D_MODEL), F32),
                pltpu.VMEM((D_MODEL, D_EXPERT), BF),
                pltpu.VMEM((D_MODEL, D_EXPERT), BF),
                pltpu.VMEM((D_EXPERT, D_MODEL), BF),
                pltpu.SemaphoreType.DMA((3,)),
            ],
        ),
        out_shape=jax.ShapeDtypeStruct((MOE_ROWS, D_MODEL), F32),
        compiler_params=_cparams(("arbitrary",)),
        name="moe_experts",
    )(tile_blk, tile_e, tile_next, n_used, xs, wg, wu, wd)


def _combine_kernel(pos_ref, ys_hbm, x1_ref, rt_ref, g2_ref, b2_ref, x2_ref, xbf_ref, buf_ref, sem):
    i = pl.program_id(0)

    def gather(tile, par):
        def start(j, c):
            t = tile * COMB_TC + j
            _row_copy(ys_hbm, buf_ref.at[par, 0], pos_ref[t], j, sem.at[par]).start()
            _row_copy(ys_hbm, buf_ref.at[par, 1], pos_ref[NTOK + t], j, sem.at[par]).start()
            return c

        lax.fori_loop(0, COMB_TC, start, 0, unroll=8)

    @pl.when(i == 0)
    def _():
        gather(0, 0)

    @pl.when(i + 1 < pl.num_programs(0))
    def _():
        gather(i + 1, (i + 1) % 2)

    par = i % 2
    for slot in range(2):
        pltpu.make_async_copy(ys_hbm.at[pl.ds(0, COMB_TC)], buf_ref.at[par, slot], sem.at[par]).wait()
    rt = rt_ref[...]
    f = rt[:, 2:3] * buf_ref[par, 0] + rt[:, 3:4] * buf_ref[par, 1]
    x2 = _layer_norm(DN_ALPHA * x1_ref[...] + f, g2_ref[...], b2_ref[...])
    x2_ref[...] = x2
    xbf_ref[...] = x2.astype(BF)


def _combine(pos, ys, x1, rt, g2, b2, layer):
    vec = lambda w: pl.BlockSpec((None, 1, w), lambda i, pos: (layer, 0, 0))
    return pl.pallas_call(
        _combine_kernel,
        grid_spec=pltpu.PrefetchScalarGridSpec(
            num_scalar_prefetch=1,
            grid=(NTOK // COMB_TC,),
            in_specs=[
                pl.BlockSpec(memory_space=pl.ANY),
                pl.BlockSpec((COMB_TC, D_MODEL), lambda i, pos: (i, 0)),
                pl.BlockSpec((COMB_TC, LANES), lambda i, pos: (i, 0)),
                vec(D_MODEL), vec(D_MODEL),
            ],
            out_specs=[
                pl.BlockSpec((COMB_TC, D_MODEL), lambda i, pos: (i, 0)),
                pl.BlockSpec((COMB_TC, D_MODEL), lambda i, pos: (i, 0)),
            ],
            scratch_shapes=[pltpu.VMEM((2, 2, COMB_TC, D_MODEL), F32), pltpu.SemaphoreType.DMA((2,))],
        ),
        out_shape=[
            jax.ShapeDtypeStruct((NTOK, D_MODEL), F32),
            jax.ShapeDtypeStruct((NTOK, D_MODEL), BF),
        ],
        compiler_params=_cparams(("arbitrary",)),
        name="moe_combine",
    )(pos, ys, x1, rt, g2, b2)


def kernel(x_prompt, x_sample, state_ssm, state_conv, state_pool, w_in, pool_map_w, pool_map_b, pool_scale,
           conv_w, conv_b, dt_bias, a_log, d_skip, ssd_norm_w, gate_a_bias, gate_b_bias, w_pool_out, w_ssd_out,
           w_o, ln1_g, ln1_b, router_group_w, router_group_b, router_expert_w, router_expert_b, expert_w_gate,
           expert_w_up, expert_w_down, ln2_g, ln2_b):
    w_in_t = jnp.swapaxes(w_in, 1, 2)
    r_pad = jnp.zeros((DEPTH, D_MODEL, LANES - N_EXPERTS - N_EXPERT_GROUPS), F32)
    wr = jnp.concatenate([router_expert_w, router_group_w, r_pad], axis=-1)
    rb = jnp.concatenate([router_expert_b, router_group_b,
                          jnp.zeros((DEPTH, LANES - N_EXPERTS - N_EXPERT_GROUPS), F32)], axis=-1)[:, None, :]
    row = lambda v: v[:, None, :]
    head_pad = lambda v: jnp.pad(v, ((0, 0), (0, LANES - SSD_HEADS)))[:, None, :]
    dtb = head_pad(dt_bias)
    alog = head_pad(a_log)
    dsk = row(jnp.repeat(d_skip, SSD_HEAD_DIM, axis=-1))
    e_bf = (jnp.arange(SSD_INNER)[None, :] // SSD_HEAD_DIM == jnp.arange(LANES)[:, None]).astype(BF)
    wg = expert_w_gate.reshape(DEPTH, N_EXPERTS, D_MODEL, D_EXPERT)
    wu = expert_w_up.reshape(DEPTH, N_EXPERTS, D_MODEL, D_EXPERT)
    wd = expert_w_down.reshape(DEPTH, N_EXPERTS, D_EXPERT, D_MODEL)
    cs_pad = jnp.pad(state_conv, ((0, 0), (0, 0), (0, DEC_SEQ - (CONV_W - 1)), (0, 0)))
    cs_pad = cs_pad.reshape(DEPTH, NS, CONV_DIM)
    h0_all = state_ssm.reshape(DEPTH, DEC_BATCH, SSD_INNER, D_STATE)

    x = jnp.concatenate([x_prompt.reshape(NP, D_MODEL), x_sample.reshape(NS, D_MODEL)], axis=0)
    x_bf = x.astype(BF)

    ssm_p, conv_p, pool_p, conv_s, pool_s = [], [], [], [], []
    hs_all = None
    for l in range(DEPTH):
        proj = _inproj(x_bf, w_in_t, l)
        tails = [lax.slice(proj, (b * SEQ + SEQ - POOL_BUF, 0), ((b + 1) * SEQ, COL_GA)) for b in range(BATCH)]
        tails = jnp.stack(tails, 0)
        pool_p.append(tails[:, :, COL_U:COL_U + POOL_DIM])
        conv_p.append(tails[:, POOL_BUF - (CONV_W - 1):, COL_XBC:])
        proj_s = lax.slice(proj, (NP, 0), (NTOK, COL_GA)).reshape(DEC_BATCH, DEC_SEQ, COL_GA)
        u_s = proj_s[:, :, COL_U:COL_U + POOL_DIM]
        pool_s.append(jnp.concatenate([state_pool[l][:, DEC_SEQ:], u_s], axis=1))
        conv_s.append(proj_s[:, DEC_SEQ - (CONV_W - 1):, COL_XBC:])

        pmb = row(pool_map_b.reshape(DEPTH, POOL_DIM))
        a_p = _pool_prompt(proj, pool_map_w, pmb, row(pool_scale), l)
        ext_s = jnp.concatenate([jnp.zeros((DEC_BATCH, 1, POOL_DIM), F32), state_pool[l], u_s], axis=1)
        a_s = _pool_sample(ext_s.reshape(DEC_BATCH * POOL_SLAB, POOL_DIM), pool_map_w, pmb, row(pool_scale), l)
        yn_p, h_p = _ssd_prompt(proj, x_bf, w_in_t, conv_w, row(conv_b), dtb, alog, dsk, row(ssd_norm_w),
                                e_bf, l)
        ssm_p.append(h_p)
        yn_s, hs_all = _ssd_sample(proj, x_bf, w_in_t, cs_pad, h0_all, conv_w, row(conv_b), dtb, alog, dsk,
                                   row(ssd_norm_w), e_bf, hs_all, l)
        m = _merge(a_p, a_s, yn_p, yn_s, proj, w_pool_out, w_ssd_out, row(gate_a_bias), row(gate_b_bias), l)
        x1, rt, cnt = _mix(m, x, w_o, row(ln1_g), row(ln1_b), wr, rb, l)
        pos, tile_blk, tile_e, tile_next, n_used, pad_lo, counts = _route_tables(rt, cnt)
        xs = _dispatch(pos, pad_lo, counts, x1)
        ys = _experts(tile_blk, tile_e, tile_next, n_used, xs, wg, wu, wd, l)
        x, x_bf = _combine(pos, ys, x1, rt, row(ln2_g), row(ln2_b), l)

    y_prompt = x[:NP].reshape(BATCH, SEQ, D_MODEL)
    y_sample = x[NP:].reshape(DEC_BATCH, DEC_SEQ, D_MODEL)
    new_ssm_prompt = jnp.stack(ssm_p, 0).reshape(DEPTH, BATCH, SSD_HEADS, SSD_HEAD_DIM, D_STATE)
    new_ssm_sample = hs_all.reshape(DEPTH, DEC_BATCH, SSD_HEADS, SSD_HEAD_DIM, D_STATE)
    return (y_prompt, y_sample, new_ssm_prompt, jnp.stack(conv_p, 0), jnp.stack(pool_p, 0),
            new_ssm_sample, jnp.stack(conv_s, 0), jnp.stack(pool_s, 0))
```

```python
import functools

import jax
import jax.numpy as jnp
from jax import lax
from jax.experimental import pallas as pl
from jax.experimental.pallas import tpu as pltpu

F32 = jnp.float32
BF = jnp.bfloat16

D_MODEL = 2048
BATCH = 4
SEQ = 2048
DEPTH = 4
DEC_BATCH = 128
DEC_SEQ = 8
PAST_LEN = 16384
POOL_WINDOWS = (2, 4, 8, 16)
N_POOL_GROUPS = 4
POOL_DIM = D_MODEL // 2
POOL_GROUP_DIM = POOL_DIM // N_POOL_GROUPS
POOL_BUF = max(POOL_WINDOWS) - 1
SSD_INNER = D_MODEL
SSD_HEAD_DIM = 64
SSD_HEADS = SSD_INNER // SSD_HEAD_DIM
SSD_GROUPS = 4
HEADS_PER_GROUP = SSD_HEADS // SSD_GROUPS
GROUP_INNER = SSD_INNER // SSD_GROUPS
D_STATE = 128
CONV_W = 4
CONV_DIM = SSD_INNER + 2 * SSD_GROUPS * D_STATE
CHUNK = 128
N_EXPERT_GROUPS = 4
EXPERTS_PER_GROUP = 8
N_EXPERTS = N_EXPERT_GROUPS * EXPERTS_PER_GROUP
D_EXPERT = D_MODEL // 4
DN_ALPHA = (2.0 * DEPTH) ** 0.25
LN_EPS = 1e-5
RMS_EPS = 1e-5
SPLIT_POINTS = (1024, 3072, 6144, 6176, 8224)

NP = BATCH * SEQ
NS = DEC_BATCH * DEC_SEQ
NTOK = NP + NS

LANES = 128
SUBLANES = 8
VMEM_LIMIT = 56 * 1024 * 1024

COL_U = 0
COL_Z = POOL_DIM
COL_XBC = POOL_DIM + SSD_INNER
COL_GA = POOL_DIM + SSD_INNER + CONV_DIM
COL_GB = COL_GA + D_MODEL
PROJ_W = COL_GB + D_MODEL
W_DT = SPLIT_POINTS[2]
W_GA = SPLIT_POINTS[3]

NEG = -1e30

PROJ_TM = 1024
PROJ_TN = 1024
POOL_TM = 512
POOL_SB = 16
POOL_SLAB = 24
SSD_SB = 4
MERGE_TM = 1024
MERGE_TN = 512
MIX_TM = 512
MOE_T = 256
MOE_TILES = 2 * NTOK // MOE_T + N_EXPERTS
MOE_ROWS = MOE_TILES * MOE_T
DISP_TD = 512
COMB_TC = 256


def _cparams(sem, vmem=VMEM_LIMIT):
    return pltpu.CompilerParams(dimension_semantics=sem, vmem_limit_bytes=vmem)


def _dot(a, b):
    return jnp.dot(a, b, preferred_element_type=F32)


def _dot_nt(a, b):
    return lax.dot_general(a, b, (((1,), (1,)), ((), ())), preferred_element_type=F32)


def _dot_tn(a, b):
    return lax.dot_general(a, b, (((0,), (0,)), ((), ())), preferred_element_type=F32)


def _split2(v):
    hi = v.astype(BF)
    lo = (v - hi.astype(F32)).astype(BF)
    return hi, lo


def _split3(v):
    h1 = v.astype(BF)
    r1 = v - h1.astype(F32)
    h2 = r1.astype(BF)
    h3 = (r1 - h2.astype(F32)).astype(BF)
    return h1, h2, h3


def _expand(v, e_bf):
    hi, lo = _split2(v)
    return _dot(hi, e_bf) + _dot(lo, e_bf)


def _silu(x):
    return x * jax.nn.sigmoid(x)


def _softplus(x):
    return jnp.maximum(x, 0.0) + jnp.log1p(jnp.exp(-jnp.abs(x)))


def _layer_norm(x, g, b):
    mu = jnp.mean(x, axis=-1, keepdims=True)
    xc = x - mu
    var = jnp.mean(xc * xc, axis=-1, keepdims=True)
    return xc * lax.rsqrt(var + LN_EPS) * g + b


def _gated_rmsnorm(y, z, nw):
    v = y * _silu(z)
    outs = []
    for g in range(SSD_GROUPS):
        vg = v[:, g * GROUP_INNER:(g + 1) * GROUP_INNER]
        ms = jnp.mean(vg * vg, axis=-1, keepdims=True)
        outs.append(vg * lax.rsqrt(ms + RMS_EPS))
    return jnp.concatenate(outs, axis=-1) * nw


def _inproj_kernel(x_ref, wt_ref, o_ref, wbf_ref):
    @pl.when(pl.program_id(1) == 0)
    def _():
        wbf_ref[...] = wt_ref[0].astype(BF)

    o_ref[...] = _dot_nt(x_ref[...], wbf_ref[...])


def _inproj(x_bf, w_in_t, layer):
    n_main = COL_GA // PROJ_TN

    def w_rows(j, i):
        row0 = j * PROJ_TN + jnp.where(j >= n_main, W_GA - W_DT, 0)
        return (layer, pl.multiple_of(row0, W_GA - W_DT), 0)

    return pl.pallas_call(
        _inproj_kernel,
        grid=(PROJ_W // PROJ_TN, NTOK // PROJ_TM),
        in_specs=[
            pl.BlockSpec((PROJ_TM, D_MODEL), lambda j, i: (i, 0)),
            pl.BlockSpec((pl.Element(1), pl.Element(PROJ_TN), pl.Element(D_MODEL)), w_rows),
        ],
        out_specs=pl.BlockSpec((PROJ_TM, PROJ_TN), lambda j, i: (i, j)),
        out_shape=jax.ShapeDtypeStruct((NTOK, PROJ_W), F32),
        scratch_shapes=[pltpu.VMEM((PROJ_TN, D_MODEL), BF)],
        compiler_params=_cparams(("arbitrary", "arbitrary")),
        name="inproj",
    )(x_bf, w_in_t)


def _pool_group(eg, lead, u_rows, pos, win, mw, mb, sc, take):
    s = eg
    sh = 1
    while sh < win:
        s = s + pltpu.roll(s, sh, axis=0)
        sh *= 2
    s = take(s)
    cnt = jnp.minimum(pos + 1, win).astype(F32)
    d = s / cnt - u_rows
    mixed = _dot(d.astype(BF), mw.astype(BF)) + mb
    return mixed * sc


def _pool_prompt_kernel(u_ref, mw_ref, mb_ref, sc_ref, o_ref, ext_ref):
    r = pl.program_id(1)
    halo = 2 * SUBLANES

    @pl.when(r == 0)
    def _():
        ext_ref[0:halo, :] = jnp.zeros((halo, POOL_DIM), F32)

    u = u_ref[...]
    ext_ref[halo:halo + POOL_TM, :] = u
    e = ext_ref[...]
    pos = r * POOL_TM + lax.broadcasted_iota(jnp.int32, (POOL_TM, 1), 0)
    for g, win in enumerate(POOL_WINDOWS):
        cs = slice(g * POOL_GROUP_DIM, (g + 1) * POOL_GROUP_DIM)
        out = _pool_group(e[:, cs], halo, u[:, cs], pos, win, mw_ref[g], mb_ref[:, cs], sc_ref[:, cs],
                          lambda s: s[halo:, :])
        o_ref[:, cs] = out.astype(BF)
    ext_ref[0:halo, :] = u[POOL_TM - halo:, :]


def _pool_prompt(proj, mw, mb, sc, layer):
    rt = SEQ // POOL_TM
    return pl.pallas_call(
        _pool_prompt_kernel,
        grid=(BATCH, rt),
        in_specs=[
            pl.BlockSpec((POOL_TM, POOL_DIM), lambda b, r: (b * rt + r, COL_U // POOL_DIM)),
            pl.BlockSpec((None, N_POOL_GROUPS, POOL_GROUP_DIM, POOL_GROUP_DIM), lambda b, r: (layer, 0, 0, 0)),
            pl.BlockSpec((None, 1, POOL_DIM), lambda b, r: (layer, 0, 0)),
            pl.BlockSpec((None, 1, POOL_DIM), lambda b, r: (layer, 0, 0)),
        ],
        out_specs=pl.BlockSpec((POOL_TM, POOL_DIM), lambda b, r: (b * rt + r, 0)),
        out_shape=jax.ShapeDtypeStruct((NP, POOL_DIM), BF),
        scratch_shapes=[pltpu.VMEM((POOL_TM + 2 * SUBLANES, POOL_DIM), F32)],
        compiler_params=_cparams(("arbitrary", "arbitrary")),
        name="pool_prompt",
    )(proj, mw, mb, sc)


def _pool_sample_kernel(ext_ref, mw_ref, mb_ref, sc_ref, o_ref):
    e = ext_ref[...]
    rows = POOL_SB * DEC_SEQ
    first = POOL_SLAB - DEC_SEQ
    pos = PAST_LEN + (lax.broadcasted_iota(jnp.int32, (rows, 1), 0) & (DEC_SEQ - 1))

    def take(s):
        s3 = s.reshape(POOL_SB, POOL_SLAB, POOL_GROUP_DIM)[:, first:, :]
        return s3.reshape(rows, POOL_GROUP_DIM)

    for g, win in enumerate(POOL_WINDOWS):
        cs = slice(g * POOL_GROUP_DIM, (g + 1) * POOL_GROUP_DIM)
        eg = e[:, cs]
        out = _pool_group(eg, first, take(eg), pos, win, mw_ref[g], mb_ref[:, cs], sc_ref[:, cs], take)
        o_ref[:, cs] = out.astype(BF)


def _pool_sample(ext_s, mw, mb, sc, layer):
    rows = POOL_SB * DEC_SEQ
    return pl.pallas_call(
        _pool_sample_kernel,
        grid=(DEC_BATCH // POOL_SB,),
        in_specs=[
            pl.BlockSpec((POOL_SB * POOL_SLAB, POOL_DIM), lambda i: (i, 0)),
            pl.BlockSpec((None, N_POOL_GROUPS, POOL_GROUP_DIM, POOL_GROUP_DIM), lambda i: (layer, 0, 0, 0)),
            pl.BlockSpec((None, 1, POOL_DIM), lambda i: (layer, 0, 0)),
            pl.BlockSpec((None, 1, POOL_DIM), lambda i: (layer, 0, 0)),
        ],
        out_specs=pl.BlockSpec((rows, POOL_DIM), lambda i: (i, 0)),
        out_shape=jax.ShapeDtypeStruct((NS, POOL_DIM), BF),
        compiler_params=_cparams(("arbitrary",)),
        name="pool_sample",
    )(ext_s, mw, mb, sc)


def _dt_heads(xbf_ref, wdt_ref, dtb_ref):
    raw = _dot_nt(xbf_ref[...], wdt_ref[...].astype(BF))
    lane = lax.broadcasted_iota(jnp.int32, raw.shape, 1)
    return _softplus(jnp.where(lane < SSD_HEADS, raw, 0.0) + dtb_ref[...])


def _ssd_prompt_kernel(xbc_ref, z0_ref, z1_ref, xbf_ref, wdt_ref, cw_ref, cb_ref, dtb_ref, alog_ref, dsk_ref,
                       nw_ref, e_ref, yn_ref, hout_ref, ext_ref, h_ref):
    c = pl.program_id(1)
    q = CHUNK
    halo = SUBLANES

    @pl.when(c == 0)
    def _():
        ext_ref[0:halo, :] = jnp.zeros((halo, CONV_DIM), F32)
        h_ref[...] = jnp.zeros_like(h_ref)

    xbc = xbc_ref[...]
    ext_ref[halo:halo + q, :] = xbc
    e = ext_ref[...]
    acc = e * cw_ref[0:1, :]
    for k in range(1, CONV_W):
        acc = pltpu.roll(acc, 1, axis=0) + e * cw_ref[k:k + 1, :]
    ext_ref[0:halo, :] = xbc[q - halo:, :]
    conv = _silu(acc[halo:, :] + cb_ref[...])

    dt = _dt_heads(xbf_ref, wdt_ref, dtb_ref)
    a = -jnp.exp(alog_ref[...])
    da = dt * a
    ri = lax.broadcasted_iota(jnp.int32, (q, q), 0)
    ci = lax.broadcasted_iota(jnp.int32, (q, q), 1)
    causal = ri >= ci
    tril = jnp.where(causal, 1.0, 0.0).astype(BF)
    d1, d2, d3 = _split3(da)
    acum = _dot(tril, d1) + _dot(tril, d2) + _dot(tril, d3)
    acum_t = acum.T
    dt_t = dt.T
    a_last = acum[q - 1:q, :]
    e_bf = e_ref[...]
    wexp = _expand(jnp.exp(a_last - acum) * dt, e_bf)
    eexp = _expand(jnp.exp(acum), e_bf)
    lane = lax.broadcasted_iota(jnp.int32, (q, LANES), 1)
    low = lane < SSD_HEAD_DIM

    y_parts = []
    for g in range(SSD_GROUPS):
        gc = slice(g * GROUP_INNER, (g + 1) * GROUP_INNER)
        b_g = conv[:, SSD_INNER + g * D_STATE:SSD_INNER + (g + 1) * D_STATE].astype(BF)
        c0 = SSD_INNER + SSD_GROUPS * D_STATE
        c_g = conv[:, c0 + g * D_STATE:c0 + (g + 1) * D_STATE].astype(BF)
        cb = _dot_nt(c_g, b_g)
        xg = conv[:, gc]
        xg_bf = xg.astype(BF)
        yd = []
        for pr in range(HEADS_PER_GROUP // 2):
            ms = []
            for j in range(2):
                hh = g * HEADS_PER_GROUP + pr * 2 + j
                seg = acum[:, hh:hh + 1] - acum_t[hh:hh + 1, :]
                dec = jnp.exp(jnp.where(causal, seg, NEG))
                ms.append((cb * dec * dt_t[hh:hh + 1, :]).astype(BF))
            mp = jnp.concatenate(ms, axis=1)
            xp = xg_bf[:, pr * LANES:(pr + 1) * LANES]
            zero = jnp.zeros_like(xp)
            xbd = jnp.concatenate([jnp.where(low, xp, zero), jnp.where(low, zero, xp)], axis=0)
            yd.append(_dot(mp, xbd))
        yd = jnp.concatenate(yd, axis=1)
        hg = h_ref[g * GROUP_INNER:(g + 1) * GROUP_INNER, :]
        yoff = _dot_nt(c_g, hg.astype(BF)) * eexp[:, gc]
        y_parts.append(yd + yoff + xg * dsk_ref[:, gc])
        xw = (xg * wexp[:, gc]).astype(BF)
        st = _dot_tn(xw, b_g)
        for k in range(HEADS_PER_GROUP):
            hh = g * HEADS_PER_GROUP + k
            rows = slice(hh * SSD_HEAD_DIM, (hh + 1) * SSD_HEAD_DIM)
            cd = jnp.exp(a_last[:, hh:hh + 1])
            h_ref[rows, :] = h_ref[rows, :] * cd + st[k * SSD_HEAD_DIM:(k + 1) * SSD_HEAD_DIM, :]
    y = jnp.concatenate(y_parts, axis=1)
    z = jnp.concatenate([z0_ref[...], z1_ref[...]], axis=1)
    yn_ref[...] = _gated_rmsnorm(y, z, nw_ref[...]).astype(BF)

    @pl.when(c == pl.num_programs(1) - 1)
    def _():
        hout_ref[...] = h_ref[...]


def _ssd_prompt(proj, x_bf, w_in_t, cw, cb, dtb, alog, dsk, nw, e_bf, layer):
    nc = SEQ // CHUNK
    vec = lambda w: pl.BlockSpec((None, 1, w), lambda b, c: (layer, 0, 0))
    half = SSD_INNER // 2
    return pl.pallas_call(
        _ssd_prompt_kernel,
        grid=(BATCH, nc),
        in_specs=[
            pl.BlockSpec((CHUNK, CONV_DIM), lambda b, c: (b * nc + c, COL_XBC // CONV_DIM)),
            pl.BlockSpec((CHUNK, half), lambda b, c: (b * nc + c, COL_Z // half)),
            pl.BlockSpec((CHUNK, half), lambda b, c: (b * nc + c, COL_Z // half + 1)),
            pl.BlockSpec((CHUNK, D_MODEL), lambda b, c: (b * nc + c, 0)),
            pl.BlockSpec((None, LANES, D_MODEL), lambda b, c: (layer, W_DT // LANES, 0)),
            pl.BlockSpec((None, CONV_W, CONV_DIM), lambda b, c: (layer, 0, 0)),
            vec(CONV_DIM), vec(LANES), vec(LANES), vec(SSD_INNER), vec(SSD_INNER),
            pl.BlockSpec((LANES, SSD_INNER), lambda b, c: (0, 0)),
        ],
        out_specs=[
            pl.BlockSpec((CHUNK, SSD_INNER), lambda b, c: (b * nc + c, 0)),
            pl.BlockSpec((None, SSD_INNER, D_STATE), lambda b, c: (b, 0, 0)),
        ],
        out_shape=[
            jax.ShapeDtypeStruct((NP, SSD_INNER), BF),
            jax.ShapeDtypeStruct((BATCH, SSD_INNER, D_STATE), F32),
        ],
        scratch_shapes=[
            pltpu.VMEM((CHUNK + SUBLANES, CONV_DIM), F32),
            pltpu.VMEM((SSD_INNER, D_STATE), F32),
        ],
        compiler_params=_cparams(("arbitrary", "arbitrary")),
        name="ssd_prompt",
    )(proj, proj, proj, x_bf, w_in_t, cw, cb, dtb, alog, dsk, nw, e_bf)


def _ssd_sample_kernel(xbc_ref, z0_ref, z1_ref, xbf_ref, wdt_ref, cs_ref, h0_ref, cw_ref, cb_ref, dtb_ref,
                       alog_ref, dsk_ref, nw_ref, e_ref, *rest):
    yn_ref, hout_ref = rest[-2:]
    t = DEC_SEQ
    r = SSD_SB * t
    l_idx = lax.broadcasted_iota(jnp.int32, (r, 1), 0) & (t - 1)
    seq_idx = lax.broadcasted_iota(jnp.int32, (r, 1), 0) // t

    def bc(v, s):
        w = v.shape[1]
        v3 = v.reshape(SSD_SB, t, w)[:, s:s + 1, :]
        return jnp.broadcast_to(v3, (SSD_SB, t, w)).reshape(r, w)

    xbc = xbc_ref[...]
    st_rows = cs_ref[...]
    acc = None
    for k in range(CONV_W):
        m = CONV_W - 1 - k
        if m == 0:
            val = xbc
        else:
            cur = pltpu.roll(xbc, m, axis=0)
            back = CONV_W - 1 - m
            stv = st_rows if back == 0 else pltpu.roll(st_rows, r - back, axis=0)
            val = jnp.where(l_idx >= m, cur, stv)
        term = val * cw_ref[k:k + 1, :]
        acc = term if acc is None else acc + term
    conv = _silu(acc + cb_ref[...])

    dt = _dt_heads(xbf_ref, wdt_ref, dtb_ref)
    a = -jnp.exp(alog_ref[...])
    acum = dt * a
    for sh in (1, 2, 4):
        acum = acum + jnp.where(l_idx >= sh, pltpu.roll(acum, sh, axis=0), 0.0)
    a_last = bc(acum, t - 1)
    e_bf = e_ref[...]
    wexp = _expand(jnp.exp(a_last - acum) * dt, e_bf)
    eexp = _expand(jnp.exp(acum), e_bf)
    cdl = jnp.exp(a_last)

    xs = conv[:, :SSD_INNER]
    bm = conv[:, SSD_INNER:SSD_INNER + SSD_GROUPS * D_STATE]
    cm = conv[:, SSD_INNER + SSD_GROUPS * D_STATE:]
    lane = lax.broadcasted_iota(jnp.int32, (r, LANES), 1)
    grp = lane // HEADS_PER_GROUP

    ms = []
    for s in range(t):
        prod = cm * bc(bm, s)
        cb = jnp.zeros((r, LANES), F32)
        for g in range(SSD_GROUPS):
            cbg = jnp.sum(prod[:, g * D_STATE:(g + 1) * D_STATE], axis=-1, keepdims=True)
            cb = jnp.where(grp == g, cbg, cb)
        dec = jnp.exp(jnp.where(l_idx >= s, acum - bc(acum, s), NEG))
        ms.append(cb * dec * bc(dt, s))
    mexp = _expand(jnp.concatenate(ms, axis=0), e_bf)
    yd = mexp[0:r, :] * bc(xs, 0)
    for s in range(1, t):
        yd = yd + mexp[s * r:(s + 1) * r, :] * bc(xs, s)

    xw = xs * wexp
    c_bf = cm.astype(BF)
    b_bf = bm.astype(BF)
    yoff_parts = []
    for g in range(SSD_GROUPS):
        gc = slice(g * GROUP_INNER, (g + 1) * GROUP_INNER)
        sc = slice(g * D_STATE, (g + 1) * D_STATE)
        hcat = h0_ref[:, gc, :].reshape(SSD_SB * GROUP_INNER, D_STATE)
        full = _dot_nt(c_bf[:, sc], hcat.astype(BF))
        yo = jnp.zeros((r, GROUP_INNER), F32)
        for qi in range(SSD_SB):
            yo = jnp.where(seq_idx == qi, full[:, qi * GROUP_INNER:(qi + 1) * GROUP_INNER], yo)
        yoff_parts.append(yo)
        for qi in range(SSD_SB):
            xq = jnp.where(seq_idx == qi, xw[:, gc], 0.0).astype(BF)
            st = _dot_tn(xq, b_bf[:, sc])
            for k in range(HEADS_PER_GROUP):
                hh = g * HEADS_PER_GROUP + k
                rows = slice(hh * SSD_HEAD_DIM, (hh + 1) * SSD_HEAD_DIM)
                cd = cdl[qi * t:qi * t + 1, hh:hh + 1]
                hout_ref[qi, rows, :] = h0_ref[qi, rows, :] * cd + st[k * SSD_HEAD_DIM:(k + 1) * SSD_HEAD_DIM, :]
    yoff = jnp.concatenate(yoff_parts, axis=1) * eexp
    y = yd + yoff + xs * dsk_ref[...]
    z = jnp.concatenate([z0_ref[...], z1_ref[...]], axis=1)
    yn_ref[...] = _gated_rmsnorm(y, z, nw_ref[...]).astype(BF)


def _ssd_sample(proj, x_bf, w_in_t, cs_pad, h0_all, cw, cb, dtb, alog, dsk, nw, e_bf, hs_prev, layer):
    r = SSD_SB * DEC_SEQ
    base = NP // r
    half = SSD_INNER // 2
    vec = lambda w: pl.BlockSpec((None, 1, w), lambda i: (layer, 0, 0))
    args = [proj, proj, proj, x_bf, w_in_t, cs_pad, h0_all, cw, cb, dtb, alog, dsk, nw, e_bf]
    in_specs = [
        pl.BlockSpec((r, CONV_DIM), lambda i: (base + i, COL_XBC // CONV_DIM)),
        pl.BlockSpec((r, half), lambda i: (base + i, COL_Z // half)),
        pl.BlockSpec((r, half), lambda i: (base + i, COL_Z // half + 1)),
        pl.BlockSpec((r, D_MODEL), lambda i: (base + i, 0)),
        pl.BlockSpec((None, LANES, D_MODEL), lambda i: (layer, W_DT // LANES, 0)),
        pl.BlockSpec((None, r, CONV_DIM), lambda i: (layer, i, 0)),
        pl.BlockSpec((None, SSD_SB, SSD_INNER, D_STATE), lambda i: (layer, i, 0, 0)),
        pl.BlockSpec((None, CONV_W, CONV_DIM), lambda i: (layer, 0, 0)),
        vec(CONV_DIM), vec(LANES), vec(LANES), vec(SSD_INNER), vec(SSD_INNER),
        pl.BlockSpec((LANES, SSD_INNER), lambda i: (0, 0)),
    ]
    aliases = {}
    if hs_prev is not None:
        aliases[len(args)] = 1
        args.append(hs_prev)
        in_specs.append(pl.BlockSpec(memory_space=pl.ANY))
    return pl.pallas_call(
        _ssd_sample_kernel,
        grid=(DEC_BATCH // SSD_SB,),
        in_specs=in_specs,
        out_specs=[
            pl.BlockSpec((r, SSD_INNER), lambda i: (i, 0)),
            pl.BlockSpec((None, SSD_SB, SSD_INNER, D_STATE), lambda i: (layer, i, 0, 0)),
        ],
        out_shape=[
            jax.ShapeDtypeStruct((NS, SSD_INNER), BF),
            jax.ShapeDtypeStruct((DEPTH, DEC_BATCH, SSD_INNER, D_STATE), F32),
        ],
        input_output_aliases=aliases,
        compiler_params=_cparams(("arbitrary",)),
        name="ssd_sample",
    )(*args)


def _merge_kernel(ap_ref, as_ref, ynp_ref, yns_ref, ga_ref, gb_ref, wpo_ref, wso_ref, gab_ref, gbb_ref, m_ref,
                  wpo_bf, wso_bf):
    i = pl.program_id(1)

    @pl.when(i == 0)
    def _():
        wpo_bf[...] = wpo_ref[...].astype(BF)
        wso_bf[...] = wso_ref[...].astype(BF)

    is_prompt = i < NP // MERGE_TM
    a = jnp.where(is_prompt, ap_ref[...], as_ref[...])
    yn = jnp.where(is_prompt, ynp_ref[...], yns_ref[...])
    pa = _dot(a, wpo_bf[...])
    ps = _dot(yn, wso_bf[...])
    m = jax.nn.sigmoid(ga_ref[...] + gab_ref[...]) * pa + jax.nn.sigmoid(gb_ref[...] + gbb_ref[...]) * ps
    m_ref[...] = m.astype(BF)


def _merge(a_p, a_s, yn_p, yn_s, proj, wpo, wso, gab, gbb, layer):
    n_p = NP // MERGE_TM
    prompt_blk = lambda j, i: (jnp.minimum(i, n_p - 1), 0)
    sample_blk = lambda j, i: (jnp.maximum(i - n_p, 0), 0)
    vec = pl.BlockSpec((None, 1, MERGE_TN), lambda j, i: (layer, 0, j))
    return pl.pallas_call(
        _merge_kernel,
        grid=(D_MODEL // MERGE_TN, NTOK // MERGE_TM),
        in_specs=[
            pl.BlockSpec((MERGE_TM, POOL_DIM), prompt_blk),
            pl.BlockSpec((MERGE_TM, POOL_DIM), sample_blk),
            pl.BlockSpec((MERGE_TM, SSD_INNER), prompt_blk),
            pl.BlockSpec((MERGE_TM, SSD_INNER), sample_blk),
            pl.BlockSpec((MERGE_TM, MERGE_TN), lambda j, i: (i, COL_GA // MERGE_TN + j)),
            pl.BlockSpec((MERGE_TM, MERGE_TN), lambda j, i: (i, COL_GB // MERGE_TN + j)),
            pl.BlockSpec((None, POOL_DIM, MERGE_TN), lambda j, i: (layer, 0, j)),
            pl.BlockSpec((None, SSD_INNER, MERGE_TN), lambda j, i: (layer, 0, j)),
            vec, vec,
        ],
        out_specs=pl.BlockSpec((MERGE_TM, MERGE_TN), lambda j, i: (i, j)),
        out_shape=jax.ShapeDtypeStruct((NTOK, D_MODEL), BF),
        scratch_shapes=[pltpu.VMEM((POOL_DIM, MERGE_TN), BF), pltpu.VMEM((SSD_INNER, MERGE_TN), BF)],
        compiler_params=_cparams(("arbitrary", "arbitrary")),
        name="merge",
    )(a_p, a_s, yn_p, yn_s, proj, proj, wpo, wso, gab, gbb)


def _mix_kernel(m_ref, x_ref, wo_ref, g1_ref, b1_ref, wr_ref, rb_ref, x1_ref, rt_ref, cnt_ref, wo_bf, run_ref):
    @pl.when(pl.program_id(0) == 0)
    def _():
        wo_bf[...] = wo_ref[...].astype(BF)

    res = DN_ALPHA * x_ref[...] + _dot(m_ref[...], wo_bf[...])
    x1 = _layer_norm(res, g1_ref[...], b1_ref[...])
    x1_ref[...] = x1

    xh, xl = _split2(x1)
    wh, wl = _split2(wr_ref[...])
    logits = _dot(xh, wh) + _dot(xh, wl) + _dot(xl, wh) + rb_ref[...]
    tm = logits.shape[0]
    lane = lax.broadcasted_iota(jnp.int32, (tm, LANES), 1)
    big = 4 * LANES
    is_g = (lane >= N_EXPERTS) & (lane < N_EXPERTS + N_EXPERT_GROUPS)
    gl = jnp.where(is_g, logits, NEG)
    gmax = jnp.max(gl, axis=-1, keepdims=True)
    gidx = jnp.min(jnp.where(gl == gmax, lane, big), axis=-1, keepdims=True) - N_EXPERTS
    gsum = jnp.sum(jnp.where(is_g, jnp.exp(gl - gmax), 0.0), axis=-1, keepdims=True)
    gval = 1.0 / gsum
    in_grp = (lane < N_EXPERTS) & ((lane // EXPERTS_PER_GROUP) == gidx)
    el = jnp.where(in_grp, logits, NEG)
    m1 = jnp.max(el, axis=-1, keepdims=True)
    i1 = jnp.min(jnp.where(el == m1, lane, big), axis=-1, keepdims=True)
    el2 = jnp.where(lane == i1, NEG, el)
    m2 = jnp.max(el2, axis=-1, keepdims=True)
    i2 = jnp.min(jnp.where(el2 == m2, lane, big), axis=-1, keepdims=True)
    r21 = jnp.exp(m2 - m1)
    w1 = gval / (1.0 + r21)
    w2 = w1 * r21

    @pl.when(pl.program_id(0) == 0)
    def _():
        run_ref[...] = jnp.zeros_like(run_ref)

    oh1 = jnp.where(lane == i1, 1.0, 0.0)
    oh2 = jnp.where(lane == i2, 1.0, 0.0)
    ri = lax.broadcasted_iota(jnp.int32, (tm, tm), 0)
    ci = lax.broadcasted_iota(jnp.int32, (tm, tm), 1)
    before = jnp.where(ri > ci, 1.0, 0.0).astype(BF)
    base = run_ref[0:1, :]
    tot1 = jnp.sum(oh1, axis=0, keepdims=True)
    tot2 = jnp.sum(oh2, axis=0, keepdims=True)
    c1 = _dot(before, oh1.astype(BF)) + base
    c2 = _dot(before, oh2.astype(BF)) + (base + tot1)
    r1 = jnp.sum(jnp.where(lane == i1, c1, 0.0), axis=-1, keepdims=True)
    r2 = jnp.sum(jnp.where(lane == i2, c2, 0.0), axis=-1, keepdims=True)
    total = base + tot1 + tot2
    run_ref[0:1, :] = total
    cnt_ref[...] = jnp.broadcast_to(total, cnt_ref.shape)

    rt = jnp.zeros((tm, LANES), F32)
    for k, v in enumerate((i1.astype(F32), i2.astype(F32), w1, w2, r1, r2)):
        rt = jnp.where(lane == k, v, rt)
    rt_ref[...] = rt


def _mix(m, x, wo, g1, b1, wr, rb, layer):
    const = lambda shape: pl.BlockSpec((None,) + shape, lambda i: (layer,) + (0,) * len(shape),
                                       pipeline_mode=pl.Buffered(1))
    vec = lambda w: pl.BlockSpec((None, 1, w), lambda i: (layer, 0, 0))
    return pl.pallas_call(
        _mix_kernel,
        grid=(NTOK // MIX_TM,),
        in_specs=[
            pl.BlockSpec((MIX_TM, D_MODEL), lambda i: (i, 0)),
            pl.BlockSpec((MIX_TM, D_MODEL), lambda i: (i, 0)),
            const((D_MODEL, D_MODEL)),
            vec(D_MODEL), vec(D_MODEL),
            const((D_MODEL, LANES)), vec(LANES),
        ],
        out_specs=[
            pl.BlockSpec((MIX_TM, D_MODEL), lambda i: (i, 0)),
            pl.BlockSpec((MIX_TM, LANES), lambda i: (i, 0)),
            pl.BlockSpec((SUBLANES, LANES), lambda i: (i, 0)),
        ],
        out_shape=[
            jax.ShapeDtypeStruct((NTOK, D_MODEL), F32),
            jax.ShapeDtypeStruct((NTOK, LANES), F32),
            jax.ShapeDtypeStruct((NTOK // MIX_TM * SUBLANES, LANES), F32),
        ],
        scratch_shapes=[pltpu.VMEM((D_MODEL, D_MODEL), BF), pltpu.VMEM((SUBLANES, LANES), F32)],
        compiler_params=_cparams(("arbitrary",)),
        name="mix",
    )(m, x, wo, g1, b1, wr, rb)


def _route_tables(rt, cnt):
    e = jnp.concatenate([rt[:, 0], rt[:, 1]]).astype(jnp.int32)
    rank = jnp.concatenate([rt[:, 4], rt[:, 5]]).astype(jnp.int32)
    counts = cnt[-1, :N_EXPERTS].astype(jnp.int32)
    padded = ((counts + MOE_T - 1) // MOE_T) * MOE_T
    ends = jnp.cumsum(padded)
    off = ends - padded
    pos = (jnp.take(off, e) + rank).astype(jnp.int32)
    n_used = (ends[-1] // MOE_T).astype(jnp.int32)
    tiles = jnp.arange(MOE_TILES, dtype=jnp.int32)
    tile_blk = jnp.minimum(tiles, n_used - 1)
    tile_e = jnp.sum((ends[None, :] // MOE_T <= tile_blk[:, None]).astype(jnp.int32), axis=1)
    tile_e = jnp.minimum(tile_e, N_EXPERTS - 1)
    later = (tile_e[None, :] > tile_e[:, None]) & (tiles[None, :] < n_used)
    tile_next = jnp.min(jnp.where(later, tile_e[None, :], N_EXPERTS), axis=1)
    tile_next = jnp.where(tile_next == N_EXPERTS, -1, tile_next).astype(jnp.int32)
    pad_lo = (ends - MOE_T).astype(jnp.int32)
    return pos, tile_blk, tile_e, tile_next, n_used.reshape(1), pad_lo, counts.astype(jnp.int32)


def _row_copy(src, dst, si, di, sem):
    return pltpu.make_async_copy(src.at[pl.ds(si, 1)], dst.at[pl.ds(di, 1)], sem)


def _dispatch_kernel(pos_ref, padlo_ref, cnt_ref, x_ref, xs_hbm, zero_ref, sem):
    i = pl.program_id(0)

    @pl.when(i == 0)
    def _():
        zero_ref[...] = jnp.zeros_like(zero_ref)

        def zero_copy(e):
            lo = pl.multiple_of(padlo_ref[e], MOE_T)
            return pltpu.make_async_copy(zero_ref, xs_hbm.at[pl.ds(lo, MOE_T)], sem)

        def zstart(e, c):
            @pl.when(cnt_ref[e] > 0)
            def _():
                zero_copy(e).start()
            return c

        def zwait(e, c):
            @pl.when(cnt_ref[e] > 0)
            def _():
                zero_copy(e).wait()
            return c

        lax.fori_loop(0, N_EXPERTS, zstart, 0)
        lax.fori_loop(0, N_EXPERTS, zwait, 0)

    t0 = i * DISP_TD

    def start(j, c):
        t = t0 + j
        _row_copy(x_ref, xs_hbm, j, pos_ref[t], sem).start()
        _row_copy(x_ref, xs_hbm, j, pos_ref[NTOK + t], sem).start()
        return c

    lax.fori_loop(0, DISP_TD, start, 0, unroll=8)
    for _ in range(2):
        pltpu.make_async_copy(x_ref, xs_hbm.at[pl.ds(0, DISP_TD)], sem).wait()


def _dispatch(pos, pad_lo, counts, x1):
    return pl.pallas_call(
        _dispatch_kernel,
        grid_spec=pltpu.PrefetchScalarGridSpec(
            num_scalar_prefetch=3,
            grid=(NTOK // DISP_TD,),
            in_specs=[pl.BlockSpec((DISP_TD, D_MODEL), lambda i, pos, lo, cnt: (i, 0))],
            out_specs=pl.BlockSpec(memory_space=pl.ANY),
            scratch_shapes=[pltpu.VMEM((MOE_T, D_MODEL), F32), pltpu.SemaphoreType.DMA(())],
        ),
        out_shape=jax.ShapeDtypeStruct((MOE_ROWS, D_MODEL), F32),
        compiler_params=_cparams(("arbitrary",)),
        name="moe_dispatch",
    )(pos, pad_lo, counts, x1)


def _expert_kernel(blk_ref, te_ref, nxt_ref, nu_ref, x_ref, wg_hbm, wu_hbm, wd_hbm, o_ref,
                   wg_st, wu_st, wd_st, wg_bf, wu_bf, wd_bf, sem, *, layer):
    i = pl.program_id(0)
    e = te_ref[i]
    first = (i == 0) | (e != te_ref[jnp.maximum(i - 1, 0)])

    def fetch(ex):
        return (pltpu.make_async_copy(wg_hbm.at[layer, ex], wg_st, sem.at[0]),
                pltpu.make_async_copy(wu_hbm.at[layer, ex], wu_st, sem.at[1]),
                pltpu.make_async_copy(wd_hbm.at[layer, ex], wd_st, sem.at[2]))

    @pl.when(i == 0)
    def _():
        for c in fetch(e):
            c.start()

    @pl.when(first)
    def _():
        for c in fetch(e):
            c.wait()
        wg_bf[...] = wg_st[...].astype(BF)
        wu_bf[...] = wu_st[...].astype(BF)
        wd_bf[...] = wd_st[...].astype(BF)
        nxt = nxt_ref[i]

        @pl.when(nxt >= 0)
        def _():
            for c in fetch(nxt):
                c.start()

    @pl.when(i < nu_ref[0])
    def _():
        x = x_ref[...].astype(BF)
        h = _silu(_dot(x, wg_bf[...])) * _dot(x, wu_bf[...])
        o_ref[...] = _dot(h.astype(BF), wd_bf[...])


def _experts(tile_blk, tile_e, tile_next, n_used, xs, wg, wu, wd, layer):
    hbm = pl.BlockSpec(memory_space=pl.ANY)
    return pl.pallas_call(
        functools.partial(_expert_kernel, layer=layer),
        grid_spec=pltpu.PrefetchScalarGridSpec(
            num_scalar_prefetch=4,
            grid=(MOE_TILES,),
            in_specs=[pl.BlockSpec((MOE_T, D_MODEL), lambda i, blk, te, nxt, nu: (blk[i], 0)), hbm, hbm, hbm],
            out_specs=pl.BlockSpec((MOE_T, D_MODEL), lambda i, blk, te, nxt, nu: (blk[i], 0)),
            scratch_shapes=[
                pltpu.VMEM((D_MODEL, D_EXPERT), F32),
                pltpu.VMEM((D_MODEL, D_EXPERT), F32),
                pltpu.VMEM((D_EXPERT, D_MODEL), F32),
                pltpu.VMEM((D_MODEL, D_EXPERT), BF),
                pltpu.VMEM((D_MODEL, D_EXPERT), BF),
                pltpu.VMEM((D_EXPERT, D_MODEL), BF),
                pltpu.SemaphoreType.DMA((3,)),
            ],
        ),
        out_shape=jax.ShapeDtypeStruct((MOE_ROWS, D_MODEL), F32),
        compiler_params=_cparams(("arbitrary",)),
        name="moe_experts",
    )(tile_blk, tile_e, tile_next, n_used, xs, wg, wu, wd)


def _combine_kernel(pos_ref, ys_hbm, x1_ref, rt_ref, g2_ref, b2_ref, x2_ref, xbf_ref, buf_ref, sem):
    i = pl.program_id(0)

    def gather(tile, par):
        def start(j, c):
            t = tile * COMB_TC + j
            _row_copy(ys_hbm, buf_ref.at[par, 0], pos_ref[t], j, sem.at[par]).start()
            _row_copy(ys_hbm, buf_ref.at[par, 1], pos_ref[NTOK + t], j, sem.at[par]).start()
            return c

        lax.fori_loop(0, COMB_TC, start, 0, unroll=8)

    @pl.when(i == 0)
    def _():
        gather(0, 0)

    @pl.when(i + 1 < pl.num_programs(0))
    def _():
        gather(i + 1, (i + 1) % 2)

    par = i % 2
    for slot in range(2):
        pltpu.make_async_copy(ys_hbm.at[pl.ds(0, COMB_TC)], buf_ref.at[par, slot], sem.at[par]).wait()
    rt = rt_ref[...]
    f = rt[:, 2:3] * buf_ref[par, 0] + rt[:, 3:4] * buf_ref[par, 1]
    x2 = _layer_norm(DN_ALPHA * x1_ref[...] + f, g2_ref[...], b2_ref[...])
    x2_ref[...] = x2
    xbf_ref[...] = x2.astype(BF)


def _combine(pos, ys, x1, rt, g2, b2, layer):
    vec = lambda w: pl.BlockSpec((None, 1, w), lambda i, pos: (layer, 0, 0))
    return pl.pallas_call(
        _combine_kernel,
        grid_spec=pltpu.PrefetchScalarGridSpec(
            num_scalar_prefetch=1,
            grid=(NTOK // COMB_TC,),
            in_specs=[
                pl.BlockSpec(memory_space=pl.ANY),
                pl.BlockSpec((COMB_TC, D_MODEL), lambda i, pos: (i, 0)),
                pl.BlockSpec((COMB_TC, LANES), lambda i, pos: (i, 0)),
                vec(D_MODEL), vec(D_MODEL),
            ],
            out_specs=[
                pl.BlockSpec((COMB_TC, D_MODEL), lambda i, pos: (i, 0)),
                pl.BlockSpec((COMB_TC, D_MODEL), lambda i, pos: (i, 0)),
            ],
            scratch_shapes=[pltpu.VMEM((2, 2, COMB_TC, D_MODEL), F32), pltpu.SemaphoreType.DMA((2,))],
        ),
        out_shape=[
            jax.ShapeDtypeStruct((NTOK, D_MODEL), F32),
            jax.ShapeDtypeStruct((NTOK, D_MODEL), BF),
        ],
        compiler_params=_cparams(("arbitrary",)),
        name="moe_combine",
    )(pos, ys, x1, rt, g2, b2)


def kernel(x_prompt, x_sample, state_ssm, state_conv, state_pool, w_in, pool_map_w, pool_map_b, pool_scale,
           conv_w, conv_b, dt_bias, a_log, d_skip, ssd_norm_w, gate_a_bias, gate_b_bias, w_pool_out, w_ssd_out,
           w_o, ln1_g, ln1_b, router_group_w, router_group_b, router_expert_w, router_expert_b, expert_w_gate,
           expert_w_up, expert_w_down, ln2_g, ln2_b):
    w_in_t = jnp.swapaxes(w_in, 1, 2)
    r_pad = jnp.zeros((DEPTH, D_MODEL, LANES - N_EXPERTS - N_EXPERT_GROUPS), F32)
    wr = jnp.concatenate([router_expert_w, router_group_w, r_pad], axis=-1)
    rb = jnp.concatenate([router_expert_b, router_group_b,
                          jnp.zeros((DEPTH, LANES - N_EXPERTS - N_EXPERT_GROUPS), F32)], axis=-1)[:, None, :]
    row = lambda v: v[:, None, :]
    head_pad = lambda v: jnp.pad(v, ((0, 0), (0, LANES - SSD_HEADS)))[:, None, :]
    dtb = head_pad(dt_bias)
    alog = head_pad(a_log)
    dsk = row(jnp.repeat(d_skip, SSD_HEAD_DIM, axis=-1))
    e_bf = (jnp.arange(SSD_INNER)[None, :] // SSD_HEAD_DIM == jnp.arange(LANES)[:, None]).astype(BF)
    wg = expert_w_gate.reshape(DEPTH, N_EXPERTS, D_MODEL, D_EXPERT)
    wu = expert_w_up.reshape(DEPTH, N_EXPERTS, D_MODEL, D_EXPERT)
    wd = expert_w_down.reshape(DEPTH, N_EXPERTS, D_EXPERT, D_MODEL)
    cs_pad = jnp.pad(state_conv, ((0, 0), (0, 0), (0, DEC_SEQ - (CONV_W - 1)), (0, 0)))
    cs_pad = cs_pad.reshape(DEPTH, NS, CONV_DIM)
    h0_all = state_ssm.reshape(DEPTH, DEC_BATCH, SSD_INNER, D_STATE)

    x = jnp.concatenate([x_prompt.reshape(NP, D_MODEL), x_sample.reshape(NS, D_MODEL)], axis=0)
    x_bf = x.astype(BF)

    ssm_p, conv_p, pool_p, conv_s, pool_s = [], [], [], [], []
    hs_all = None
    for l in range(DEPTH):
        proj = _inproj(x_bf, w_in_t, l)
        tails = [lax.slice(proj, (b * SEQ + SEQ - POOL_BUF, 0), ((b + 1) * SEQ, COL_GA)) for b in range(BATCH)]
        tails = jnp.stack(tails, 0)
        pool_p.append(tails[:, :, COL_U:COL_U + POOL_DIM])
        conv_p.append(tails[:, POOL_BUF - (CONV_W - 1):, COL_XBC:])
        proj_s = lax.slice(proj, (NP, 0), (NTOK, COL_GA)).reshape(DEC_BATCH, DEC_SEQ, COL_GA)
        u_s = proj_s[:, :, COL_U:COL_U + POOL_DIM]
        pool_s.append(jnp.concatenate([state_pool[l][:, DEC_SEQ:], u_s], axis=1))
        conv_s.append(proj_s[:, DEC_SEQ - (CONV_W - 1):, COL_XBC:])

        pmb = row(pool_map_b.reshape(DEPTH, POOL_DIM))
        a_p = _pool_prompt(proj, pool_map_w, pmb, row(pool_scale), l)
        ext_s = jnp.concatenate([jnp.zeros((DEC_BATCH, 1, POOL_DIM), F32), state_pool[l], u_s], axis=1)
        a_s = _pool_sample(ext_s.reshape(DEC_BATCH * POOL_SLAB, POOL_DIM), pool_map_w, pmb, row(pool_scale), l)
        yn_p, h_p = _ssd_prompt(proj, x_bf, w_in_t, conv_w, row(conv_b), dtb, alog, dsk, row(ssd_norm_w),
                                e_bf, l)
        ssm_p.append(h_p)
        yn_s, hs_all = _ssd_sample(proj, x_bf, w_in_t, cs_pad, h0_all, conv_w, row(conv_b), dtb, alog, dsk,
                                   row(ssd_norm_w), e_bf, hs_all, l)
        m = _merge(a_p, a_s, yn_p, yn_s, proj, w_pool_out, w_ssd_out, row(gate_a_bias), row(gate_b_bias), l)
        x1, rt, cnt = _mix(m, x, w_o, row(ln1_g), row(ln1_b), wr, rb, l)
        pos, tile_blk, tile_e, tile_next, n_used, pad_lo, counts = _route_tables(rt, cnt)
        xs = _dispatch(pos, pad_lo, counts, x1)
        ys = _experts(tile_blk, tile_e, tile_next, n_used, xs, wg, wu, wd, l)
        x, x_bf = _combine(pos, ys, x1, rt, row(ln2_g), row(ln2_b), l)

    y_prompt = x[:NP].reshape(BATCH, SEQ, D_MODEL)
    y_sample = x[NP:].reshape(DEC_BATCH, DEC_SEQ, D_MODEL)
    new_ssm_prompt = jnp.stack(ssm_p, 0).reshape(DEPTH, BATCH, SSD_HEADS, SSD_HEAD_DIM, D_STATE)
    new_ssm_sample = hs_all.reshape(DEPTH, DEC_BATCH, SSD_HEADS, SSD_HEAD_DIM, D_STATE)
    return (y_prompt, y_sample, new_ssm_prompt, jnp.stack(conv_p, 0), jnp.stack(pool_p, 0),
            new_ssm_sample, jnp.stack(conv_s, 0), jnp.stack(pool_s, 0))
```

```python
import functools

import jax
import jax.numpy as jnp
from jax import lax
from jax.experimental import pallas as pl
from jax.experimental.pallas import tpu as pltpu

F32 = jnp.float32
BF = jnp.bfloat16

D_MODEL = 2048
BATCH = 4
SEQ = 2048
DEPTH = 4
DEC_BATCH = 128
DEC_SEQ = 8
PAST_LEN = 16384
POOL_WINDOWS = (2, 4, 8, 16)
N_POOL_GROUPS = 4
POOL_DIM = D_MODEL // 2
POOL_GROUP_DIM = POOL_DIM // N_POOL_GROUPS
POOL_BUF = max(POOL_WINDOWS) - 1
SSD_INNER = D_MODEL
SSD_HEAD_DIM = 64
SSD_HEADS = SSD_INNER // SSD_HEAD_DIM
SSD_GROUPS = 4
HEADS_PER_GROUP = SSD_HEADS // SSD_GROUPS
GROUP_INNER = SSD_INNER // SSD_GROUPS
D_STATE = 128
CONV_W = 4
CONV_DIM = SSD_INNER + 2 * SSD_GROUPS * D_STATE
CHUNK = 128
N_EXPERT_GROUPS = 4
EXPERTS_PER_GROUP = 8
N_EXPERTS = N_EXPERT_GROUPS * EXPERTS_PER_GROUP
D_EXPERT = D_MODEL // 4
DN_ALPHA = (2.0 * DEPTH) ** 0.25
LN_EPS = 1e-5
RMS_EPS = 1e-5
SPLIT_POINTS = (1024, 3072, 6144, 6176, 8224)

NP = BATCH * SEQ
NS = DEC_BATCH * DEC_SEQ
NTOK = NP + NS

LANES = 128
SUBLANES = 8
VMEM_LIMIT = 56 * 1024 * 1024

COL_U = 0
COL_Z = POOL_DIM
COL_XBC = POOL_DIM + SSD_INNER
COL_GA = POOL_DIM + SSD_INNER + CONV_DIM
COL_GB = COL_GA + D_MODEL
PROJ_W = COL_GB + D_MODEL
W_DT = SPLIT_POINTS[2]
W_GA = SPLIT_POINTS[3]

NEG = -1e30

PROJ_TM = 1024
PROJ_TN = 1024
POOL_TM = 512
POOL_SB = 16
POOL_SLAB = 24
SSD_SB = 4
MERGE_TM = 1024
MERGE_TN = 512
MIX_TM = 512
MOE_T = 256
MOE_TILES = 2 * NTOK // MOE_T + N_EXPERTS
MOE_ROWS = MOE_TILES * MOE_T
DISP_TD = 512
COMB_TC = 256


def _cparams(sem, vmem=VMEM_LIMIT):
    return pltpu.CompilerParams(dimension_semantics=sem, vmem_limit_bytes=vmem)


def _dot(a, b):
    return jnp.dot(a, b, preferred_element_type=F32)


def _dot_nt(a, b):
    return lax.dot_general(a, b, (((1,), (1,)), ((), ())), preferred_element_type=F32)


def _dot_tn(a, b):
    return lax.dot_general(a, b, (((0,), (0,)), ((), ())), preferred_element_type=F32)


def _split2(v):
    hi = v.astype(BF)
    lo = (v - hi.astype(F32)).astype(BF)
    return hi, lo


def _split3(v):
    h1 = v.astype(BF)
    r1 = v - h1.astype(F32)
    h2 = r1.astype(BF)
    h3 = (r1 - h2.astype(F32)).astype(BF)
    return h1, h2, h3


def _expand(v, e_bf):
    hi, lo = _split2(v)
    return _dot(hi, e_bf) + _dot(lo, e_bf)


def _silu(x):
    return x * jax.nn.sigmoid(x)


def _softplus(x):
    return jnp.maximum(x, 0.0) + jnp.log1p(jnp.exp(-jnp.abs(x)))


def _layer_norm(x, g, b):
    mu = jnp.mean(x, axis=-1, keepdims=True)
    xc = x - mu
    var = jnp.mean(xc * xc, axis=-1, keepdims=True)
    return xc * lax.rsqrt(var + LN_EPS) * g + b


def _gated_rmsnorm(y, z, nw):
    v = y * _silu(z)
    outs = []
    for g in range(SSD_GROUPS):
        vg = v[:, g * GROUP_INNER:(g + 1) * GROUP_INNER]
        ms = jnp.mean(vg * vg, axis=-1, keepdims=True)
        outs.append(vg * lax.rsqrt(ms + RMS_EPS))
    return jnp.concatenate(outs, axis=-1) * nw


def _inproj_kernel(x_ref, wt_ref, o_ref, wbf_ref):
    @pl.when(pl.program_id(1) == 0)
    def _():
        wbf_ref[...] = wt_ref[0].astype(BF)

    o_ref[...] = _dot_nt(x_ref[...], wbf_ref[...])


def _inproj(x_bf, w_in_t, layer):
    n_main = COL_GA // PROJ_TN

    def w_rows(j, i):
        row0 = j * PROJ_TN + jnp.where(j >= n_main, W_GA - W_DT, 0)
        return (layer, pl.multiple_of(row0, W_GA - W_DT), 0)

    return pl.pallas_call(
        _inproj_kernel,
        grid=(PROJ_W // PROJ_TN, NTOK // PROJ_TM),
        in_specs=[
            pl.BlockSpec((PROJ_TM, D_MODEL), lambda j, i: (i, 0)),
            pl.BlockSpec((pl.Element(1), pl.Element(PROJ_TN), pl.Element(D_MODEL)), w_rows),
        ],
        out_specs=pl.BlockSpec((PROJ_TM, PROJ_TN), lambda j, i: (i, j)),
        out_shape=jax.ShapeDtypeStruct((NTOK, PROJ_W), F32),
        scratch_shapes=[pltpu.VMEM((PROJ_TN, D_MODEL), BF)],
        compiler_params=_cparams(("arbitrary", "arbitrary")),
        name="inproj",
    )(x_bf, w_in_t)


def _pool_group(eg, lead, u_rows, pos, win, mw, mb, sc, take):
    s = eg
    sh = 1
    while sh < win:
        s = s + pltpu.roll(s, sh, axis=0)
        sh *= 2
    s = take(s)
    cnt = jnp.minimum(pos + 1, win).astype(F32)
    d = s / cnt - u_rows
    mixed = _dot(d.astype(BF), mw.astype(BF)) + mb
    return mixed * sc


def _pool_prompt_kernel(u_ref, mw_ref, mb_ref, sc_ref, o_ref, ext_ref):
    r = pl.program_id(1)
    halo = 2 * SUBLANES

    @pl.when(r == 0)
    def _():
        ext_ref[0:halo, :] = jnp.zeros((halo, POOL_DIM), F32)

    u = u_ref[...]
    ext_ref[halo:halo + POOL_TM, :] = u
    e = ext_ref[...]
    pos = r * POOL_TM + lax.broadcasted_iota(jnp.int32, (POOL_TM, 1), 0)
    for g, win in enumerate(POOL_WINDOWS):
        cs = slice(g * POOL_GROUP_DIM, (g + 1) * POOL_GROUP_DIM)
        out = _pool_group(e[:, cs], halo, u[:, cs], pos, win, mw_ref[g], mb_ref[:, cs], sc_ref[:, cs],
                          lambda s: s[halo:, :])
        o_ref[:, cs] = out.astype(BF)
    ext_ref[0:halo, :] = u[POOL_TM - halo:, :]


def _pool_prompt(proj, mw, mb, sc, layer):
    rt = SEQ // POOL_TM
    return pl.pallas_call(
        _pool_prompt_kernel,
        grid=(BATCH, rt),
        in_specs=[
            pl.BlockSpec((POOL_TM, POOL_DIM), lambda b, r: (b * rt + r, COL_U // POOL_DIM)),
            pl.BlockSpec((None, N_POOL_GROUPS, POOL_GROUP_DIM, POOL_GROUP_DIM), lambda b, r: (layer, 0, 0, 0)),
            pl.BlockSpec((None, 1, POOL_DIM), lambda b, r: (layer, 0, 0)),
            pl.BlockSpec((None, 1, POOL_DIM), lambda b, r: (layer, 0, 0)),
        ],
        out_specs=pl.BlockSpec((POOL_TM, POOL_DIM), lambda b, r: (b * rt + r, 0)),
        out_shape=jax.ShapeDtypeStruct((NP, POOL_DIM), BF),
        scratch_shapes=[pltpu.VMEM((POOL_TM + 2 * SUBLANES, POOL_DIM), F32)],
        compiler_params=_cparams(("arbitrary", "arbitrary")),
        name="pool_prompt",
    )(proj, mw, mb, sc)


def _pool_sample_kernel(ext_ref, mw_ref, mb_ref, sc_ref, o_ref):
    e = ext_ref[...]
    rows = POOL_SB * DEC_SEQ
    first = POOL_SLAB - DEC_SEQ
    pos = PAST_LEN + (lax.broadcasted_iota(jnp.int32, (rows, 1), 0) & (DEC_SEQ - 1))

    def take(s):
        s3 = s.reshape(POOL_SB, POOL_SLAB, POOL_GROUP_DIM)[:, first:, :]
        return s3.reshape(rows, POOL_GROUP_DIM)

    for g, win in enumerate(POOL_WINDOWS):
        cs = slice(g * POOL_GROUP_DIM, (g + 1) * POOL_GROUP_DIM)
        eg = e[:, cs]
        out = _pool_group(eg, first, take(eg), pos, win, mw_ref[g], mb_ref[:, cs], sc_ref[:, cs], take)
        o_ref[:, cs] = out.astype(BF)


def _pool_sample(ext_s, mw, mb, sc, layer):
    rows = POOL_SB * DEC_SEQ
    return pl.pallas_call(
        _pool_sample_kernel,
        grid=(DEC_BATCH // POOL_SB,),
        in_specs=[
            pl.BlockSpec((POOL_SB * POOL_SLAB, POOL_DIM), lambda i: (i, 0)),
            pl.BlockSpec((None, N_POOL_GROUPS, POOL_GROUP_DIM, POOL_GROUP_DIM), lambda i: (layer, 0, 0, 0)),
            pl.BlockSpec((None, 1, POOL_DIM), lambda i: (layer, 0, 0)),
            pl.BlockSpec((None, 1, POOL_DIM), lambda i: (layer, 0, 0)),
        ],
        out_specs=pl.BlockSpec((rows, POOL_DIM), lambda i: (i, 0)),
        out_shape=jax.ShapeDtypeStruct((NS, POOL_DIM), BF),
        compiler_params=_cparams(("arbitrary",)),
        name="pool_sample",
    )(ext_s, mw, mb, sc)


def _dt_heads(xbf_ref, wdt_ref, dtb_ref):
    raw = _dot_nt(xbf_ref[...], wdt_ref[...].astype(BF))
    lane = lax.broadcasted_iota(jnp.int32, raw.shape, 1)
    return _softplus(jnp.where(lane < SSD_HEADS, raw, 0.0) + dtb_ref[...])


def _ssd_prompt_kernel(xbc_ref, z0_ref, z1_ref, xbf_ref, wdt_ref, cw_ref, cb_ref, dtb_ref, alog_ref, dsk_ref,
                       nw_ref, e_ref, yn_ref, hout_ref, ext_ref, h_ref):
    c = pl.program_id(1)
    q = CHUNK
    halo = SUBLANES

    @pl.when(c == 0)
    def _():
        ext_ref[0:halo, :] = jnp.zeros((halo, CONV_DIM), F32)
        h_ref[...] = jnp.zeros_like(h_ref)

    xbc = xbc_ref[...]
    ext_ref[halo:halo + q, :] = xbc
    e = ext_ref[...]
    acc = e * cw_ref[0:1, :]
    for k in range(1, CONV_W):
        acc = pltpu.roll(acc, 1, axis=0) + e * cw_ref[k:k + 1, :]
    ext_ref[0:halo, :] = xbc[q - halo:, :]
    conv = _silu(acc[halo:, :] + cb_ref[...])

    dt = _dt_heads(xbf_ref, wdt_ref, dtb_ref)
    a = -jnp.exp(alog_ref[...])
    da = dt * a
    ri = lax.broadcasted_iota(jnp.int32, (q, q), 0)
    ci = lax.broadcasted_iota(jnp.int32, (q, q), 1)
    causal = ri >= ci
    tril = jnp.where(causal, 1.0, 0.0).astype(BF)
    d1, d2, d3 = _split3(da)
    acum = _dot(tril, d1) + _dot(tril, d2) + _dot(tril, d3)
    acum_t = acum.T
    dt_t = dt.T
    a_last = acum[q - 1:q, :]
    e_bf = e_ref[...]
    wexp = _expand(jnp.exp(a_last - acum) * dt, e_bf)
    eexp = _expand(jnp.exp(acum), e_bf)
    lane = lax.broadcasted_iota(jnp.int32, (q, LANES), 1)
    low = lane < SSD_HEAD_DIM

    y_parts = []
    for g in range(SSD_GROUPS):
        gc = slice(g * GROUP_INNER, (g + 1) * GROUP_INNER)
        b_g = conv[:, SSD_INNER + g * D_STATE:SSD_INNER + (g + 1) * D_STATE].astype(BF)
        c0 = SSD_INNER + SSD_GROUPS * D_STATE
        c_g = conv[:, c0 + g * D_STATE:c0 + (g + 1) * D_STATE].astype(BF)
        cb = _dot_nt(c_g, b_g)
        xg = conv[:, gc]
        xg_bf = xg.astype(BF)
        yd = []
        for pr in range(HEADS_PER_GROUP // 2):
            ms = []
            for j in range(2):
                hh = g * HEADS_PER_GROUP + pr * 2 + j
                seg = acum[:, hh:hh + 1] - acum_t[hh:hh + 1, :]
                dec = jnp.exp(jnp.where(causal, seg, NEG))
                ms.append((cb * dec * dt_t[hh:hh + 1, :]).astype(BF))
            mp = jnp.concatenate(ms, axis=1)
            xp = xg_bf[:, pr * LANES:(pr + 1) * LANES]
            zero = jnp.zeros_like(xp)
            xbd = jnp.concatenate([jnp.where(low, xp, zero), jnp.where(low, zero, xp)], axis=0)
            yd.append(_dot(mp, xbd))
        yd = jnp.concatenate(yd, axis=1)
        hg = h_ref[g * GROUP_INNER:(g + 1) * GROUP_INNER, :]
        yoff = _dot_nt(c_g, hg.astype(BF)) * eexp[:, gc]
        y_parts.append(yd + yoff + xg * dsk_ref[:, gc])
        xw = (xg * wexp[:, gc]).astype(BF)
        st = _dot_tn(xw, b_g)
        for k in range(HEADS_PER_GROUP):
            hh = g * HEADS_PER_GROUP + k
            rows = slice(hh * SSD_HEAD_DIM, (hh + 1) * SSD_HEAD_DIM)
            cd = jnp.exp(a_last[:, hh:hh + 1])
            h_ref[rows, :] = h_ref[rows, :] * cd + st[k * SSD_HEAD_DIM:(k + 1) * SSD_HEAD_DIM, :]
    y = jnp.concatenate(y_parts, axis=1)
    z = jnp.concatenate([z0_ref[...], z1_ref[...]], axis=1)
    yn_ref[...] = _gated_rmsnorm(y, z, nw_ref[...]).astype(BF)

    @pl.when(c == pl.num_programs(1) - 1)
    def _():
        hout_ref[...] = h_ref[...]


def _ssd_prompt(proj, x_bf, w_in_t, cw, cb, dtb, alog, dsk, nw, e_bf, layer):
    nc = SEQ // CHUNK
    vec = lambda w: pl.BlockSpec((None, 1, w), lambda b, c: (layer, 0, 0))
    half = SSD_INNER // 2
    return pl.pallas_call(
        _ssd_prompt_kernel,
        grid=(BATCH, nc),
        in_specs=[
            pl.BlockSpec((CHUNK, CONV_DIM), lambda b, c: (b * nc + c, COL_XBC // CONV_DIM)),
            pl.BlockSpec((CHUNK, half), lambda b, c: (b * nc + c, COL_Z // half)),
            pl.BlockSpec((CHUNK, half), lambda b, c: (b * nc + c, COL_Z // half + 1)),
            pl.BlockSpec((CHUNK, D_MODEL), lambda b, c: (b * nc + c, 0)),
            pl.BlockSpec((None, LANES, D_MODEL), lambda b, c: (layer, W_DT // LANES, 0)),
            pl.BlockSpec((None, CONV_W, CONV_DIM), lambda b, c: (layer, 0, 0)),
            vec(CONV_DIM), vec(LANES), vec(LANES), vec(SSD_INNER), vec(SSD_INNER),
            pl.BlockSpec((LANES, SSD_INNER), lambda b, c: (0, 0)),
        ],
        out_specs=[
            pl.BlockSpec((CHUNK, SSD_INNER), lambda b, c: (b * nc + c, 0)),
            pl.BlockSpec((None, SSD_INNER, D_STATE), lambda b, c: (b, 0, 0)),
        ],
        out_shape=[
            jax.ShapeDtypeStruct((NP, SSD_INNER), BF),
            jax.ShapeDtypeStruct((BATCH, SSD_INNER, D_STATE), F32),
        ],
        scratch_shapes=[
            pltpu.VMEM((CHUNK + SUBLANES, CONV_DIM), F32),
            pltpu.VMEM((SSD_INNER, D_STATE), F32),
        ],
        compiler_params=_cparams(("arbitrary", "arbitrary")),
        name="ssd_prompt",
    )(proj, proj, proj, x_bf, w_in_t, cw, cb, dtb, alog, dsk, nw, e_bf)


def _ssd_sample_kernel(xbc_ref, z0_ref, z1_ref, xbf_ref, wdt_ref, cs_ref, h0_ref, cw_ref, cb_ref, dtb_ref,
                       alog_ref, dsk_ref, nw_ref, e_ref, *rest):
    yn_ref, hout_ref = rest[-2:]
    t = DEC_SEQ
    r = SSD_SB * t
    l_idx = lax.broadcasted_iota(jnp.int32, (r, 1), 0) & (t - 1)
    seq_idx = lax.broadcasted_iota(jnp.int32, (r, 1), 0) // t

    def bc(v, s):
        w = v.shape[1]
        v3 = v.reshape(SSD_SB, t, w)[:, s:s + 1, :]
        return jnp.broadcast_to(v3, (SSD_SB, t, w)).reshape(r, w)

    xbc = xbc_ref[...]
    st_rows = cs_ref[...]
    acc = None
    for k in range(CONV_W):
        m = CONV_W - 1 - k
        if m == 0:
            val = xbc
        else:
            cur = pltpu.roll(xbc, m, axis=0)
            back = CONV_W - 1 - m
            stv = st_rows if back == 0 else pltpu.roll(st_rows, r - back, axis=0)
            val = jnp.where(l_idx >= m, cur, stv)
        term = val * cw_ref[k:k + 1, :]
        acc = term if acc is None else acc + term
    conv = _silu(acc + cb_ref[...])

    dt = _dt_heads(xbf_ref, wdt_ref, dtb_ref)
    a = -jnp.exp(alog_ref[...])
    acum = dt * a
    for sh in (1, 2, 4):
        acum = acum + jnp.where(l_idx >= sh, pltpu.roll(acum, sh, axis=0), 0.0)
    a_last = bc(acum, t - 1)
    e_bf = e_ref[...]
    wexp = _expand(jnp.exp(a_last - acum) * dt, e_bf)
    eexp = _expand(jnp.exp(acum), e_bf)
    cdl = jnp.exp(a_last)

    xs = conv[:, :SSD_INNER]
    bm = conv[:, SSD_INNER:SSD_INNER + SSD_GROUPS * D_STATE]
    cm = conv[:, SSD_INNER + SSD_GROUPS * D_STATE:]
    lane = lax.broadcasted_iota(jnp.int32, (r, LANES), 1)
    grp = lane // HEADS_PER_GROUP

    ms = []
    for s in range(t):
        prod = cm * bc(bm, s)
        cb = jnp.zeros((r, LANES), F32)
        for g in range(SSD_GROUPS):
            cbg = jnp.sum(prod[:, g * D_STATE:(g + 1) * D_STATE], axis=-1, keepdims=True)
            cb = jnp.where(grp == g, cbg, cb)
        dec = jnp.exp(jnp.where(l_idx >= s, acum - bc(acum, s), NEG))
        ms.append(cb * dec * bc(dt, s))
    mexp = _expand(jnp.concatenate(ms, axis=0), e_bf)
    yd = mexp[0:r, :] * bc(xs, 0)
    for s in range(1, t):
        yd = yd + mexp[s * r:(s + 1) * r, :] * bc(xs, s)

    xw = xs * wexp
    c_bf = cm.astype(BF)
    b_bf = bm.astype(BF)
    yoff_parts = []
    for g in range(SSD_GROUPS):
        gc = slice(g * GROUP_INNER, (g + 1) * GROUP_INNER)
        sc = slice(g * D_STATE, (g + 1) * D_STATE)
        hcat = h0_ref[:, gc, :].reshape(SSD_SB * GROUP_INNER, D_STATE)
        full = _dot_nt(c_bf[:, sc], hcat.astype(BF))
        yo = jnp.zeros((r, GROUP_INNER), F32)
        for qi in range(SSD_SB):
            yo = jnp.where(seq_idx == qi, full[:, qi * GROUP_INNER:(qi + 1) * GROUP_INNER], yo)
        yoff_parts.append(yo)
        for qi in range(SSD_SB):
            xq = jnp.where(seq_idx == qi, xw[:, gc], 0.0).astype(BF)
            st = _dot_tn(xq, b_bf[:, sc])
            for k in range(HEADS_PER_GROUP):
                hh = g * HEADS_PER_GROUP + k
                rows = slice(hh * SSD_HEAD_DIM, (hh + 1) * SSD_HEAD_DIM)
                cd = cdl[qi * t:qi * t + 1, hh:hh + 1]
                hout_ref[qi, rows, :] = h0_ref[qi, rows, :] * cd + st[k * SSD_HEAD_DIM:(k + 1) * SSD_HEAD_DIM, :]
    yoff = jnp.concatenate(yoff_parts, axis=1) * eexp
    y = yd + yoff + xs * dsk_ref[...]
    z = jnp.concatenate([z0_ref[...], z1_ref[...]], axis=1)
    yn_ref[...] = _gated_rmsnorm(y, z, nw_ref[...]).astype(BF)


def _ssd_sample(proj, x_bf, w_in_t, cs_pad, h0_all, cw, cb, dtb, alog, dsk, nw, e_bf, hs_prev, layer):
    r = SSD_SB * DEC_SEQ
    base = NP // r
    half = SSD_INNER // 2
    vec = lambda w: pl.BlockSpec((None, 1, w), lambda i: (layer, 0, 0))
    args = [proj, proj, proj, x_bf, w_in_t, cs_pad, h0_all, cw, cb, dtb, alog, dsk, nw, e_bf]
    in_specs = [
        pl.BlockSpec((r, CONV_DIM), lambda i: (base + i, COL_XBC // CONV_DIM)),
        pl.BlockSpec((r, half), lambda i: (base + i, COL_Z // half)),
        pl.BlockSpec((r, half), lambda i: (base + i, COL_Z // half + 1)),
        pl.BlockSpec((r, D_MODEL), lambda i: (base + i, 0)),
        pl.BlockSpec((None, LANES, D_MODEL), lambda i: (layer, W_DT // LANES, 0)),
        pl.BlockSpec((None, r, CONV_DIM), lambda i: (layer, i, 0)),
        pl.BlockSpec((None, SSD_SB, SSD_INNER, D_STATE), lambda i: (layer, i, 0, 0)),
        pl.BlockSpec((None, CONV_W, CONV_DIM), lambda i: (layer, 0, 0)),
        vec(CONV_DIM), vec(LANES), vec(LANES), vec(SSD_INNER), vec(SSD_INNER),
        pl.BlockSpec((LANES, SSD_INNER), lambda i: (0, 0)),
    ]
    aliases = {}
    if hs_prev is not None:
        aliases[len(args)] = 1
        args.append(hs_prev)
        in_specs.append(pl.BlockSpec(memory_space=pl.ANY))
    return pl.pallas_call(
        _ssd_sample_kernel,
        grid=(DEC_BATCH // SSD_SB,),
        in_specs=in_specs,
        out_specs=[
            pl.BlockSpec((r, SSD_INNER), lambda i: (i, 0)),
            pl.BlockSpec((None, SSD_SB, SSD_INNER, D_STATE), lambda i: (layer, i, 0, 0)),
        ],
        out_shape=[
            jax.ShapeDtypeStruct((NS, SSD_INNER), BF),
            jax.ShapeDtypeStruct((DEPTH, DEC_BATCH, SSD_INNER, D_STATE), F32),
        ],
        input_output_aliases=aliases,
        compiler_params=_cparams(("arbitrary",)),
        name="ssd_sample",
    )(*args)


def _merge_kernel(ap_ref, as_ref, ynp_ref, yns_ref, ga_ref, gb_ref, wpo_ref, wso_ref, gab_ref, gbb_ref, m_ref,
                  wpo_bf, wso_bf):
    i = pl.program_id(1)

    @pl.when(i == 0)
    def _():
        wpo_bf[...] = wpo_ref[...].astype(BF)
        wso_bf[...] = wso_ref[...].astype(BF)

    is_prompt = i < NP // MERGE_TM
    a = jnp.where(is_prompt, ap_ref[...], as_ref[...])
    yn = jnp.where(is_prompt, ynp_ref[...], yns_ref[...])
    pa = _dot(a, wpo_bf[...])
    ps = _dot(yn, wso_bf[...])
    m = jax.nn.sigmoid(ga_ref[...] + gab_ref[...]) * pa + jax.nn.sigmoid(gb_ref[...] + gbb_ref[...]) * ps
    m_ref[...] = m.astype(BF)


def _merge(a_p, a_s, yn_p, yn_s, proj, wpo, wso, gab, gbb, layer):
    n_p = NP // MERGE_TM
    prompt_blk = lambda j, i: (jnp.minimum(i, n_p - 1), 0)
    sample_blk = lambda j, i: (jnp.maximum(i - n_p, 0), 0)
    vec = pl.BlockSpec((None, 1, MERGE_TN), lambda j, i: (layer, 0, j))
    return pl.pallas_call(
        _merge_kernel,
        grid=(D_MODEL // MERGE_TN, NTOK // MERGE_TM),
        in_specs=[
            pl.BlockSpec((MERGE_TM, POOL_DIM), prompt_blk),
            pl.BlockSpec((MERGE_TM, POOL_DIM), sample_blk),
            pl.BlockSpec((MERGE_TM, SSD_INNER), prompt_blk),
            pl.BlockSpec((MERGE_TM, SSD_INNER), sample_blk),
            pl.BlockSpec((MERGE_TM, MERGE_TN), lambda j, i: (i, COL_GA // MERGE_TN + j)),
            pl.BlockSpec((MERGE_TM, MERGE_TN), lambda j, i: (i, COL_GB // MERGE_TN + j)),
            pl.BlockSpec((None, POOL_DIM, MERGE_TN), lambda j, i: (layer, 0, j)),
            pl.BlockSpec((None, SSD_INNER, MERGE_TN), lambda j, i: (layer, 0, j)),
            vec, vec,
        ],
        out_specs=pl.BlockSpec((MERGE_TM, MERGE_TN), lambda j, i: (i, j)),
        out_shape=jax.ShapeDtypeStruct((NTOK, D_MODEL), BF),
        scratch_shapes=[pltpu.VMEM((POOL_DIM, MERGE_TN), BF), pltpu.VMEM((SSD_INNER, MERGE_TN), BF)],
        compiler_params=_cparams(("arbitrary", "arbitrary")),
        name="merge",
    )(a_p, a_s, yn_p, yn_s, proj, proj, wpo, wso, gab, gbb)


def _mix_kernel(m_ref, x_ref, wo_ref, g1_ref, b1_ref, wr_ref, rb_ref, x1_ref, rt_ref, cnt_ref, wo_bf, run_ref):
    @pl.when(pl.program_id(0) == 0)
    def _():
        wo_bf[...] = wo_ref[...].astype(BF)

    res = DN_ALPHA * x_ref[...] + _dot(m_ref[...], wo_bf[...])
    x1 = _layer_norm(res, g1_ref[...], b1_ref[...])
    x1_ref[...] = x1

    xh, xl = _split2(x1)
    wh, wl = _split2(wr_ref[...])
    logits = _dot(xh, wh) + _dot(xh, wl) + _dot(xl, wh) + rb_ref[...]
    tm = logits.shape[0]
    lane = lax.broadcasted_iota(jnp.int32, (tm, LANES), 1)
    big = 4 * LANES
    is_g = (lane >= N_EXPERTS) & (lane < N_EXPERTS + N_EXPERT_GROUPS)
    gl = jnp.where(is_g, logits, NEG)
    gmax = jnp.max(gl, axis=-1, keepdims=True)
    gidx = jnp.min(jnp.where(gl == gmax, lane, big), axis=-1, keepdims=True) - N_EXPERTS
    gsum = jnp.sum(jnp.where(is_g, jnp.exp(gl - gmax), 0.0), axis=-1, keepdims=True)
    gval = 1.0 / gsum
    in_grp = (lane < N_EXPERTS) & ((lane // EXPERTS_PER_GROUP) == gidx)
    el = jnp.where(in_grp, logits, NEG)
    m1 = jnp.max(el, axis=-1, keepdims=True)
    i1 = jnp.min(jnp.where(el == m1, lane, big), axis=-1, keepdims=True)
    el2 = jnp.where(lane == i1, NEG, el)
    m2 = jnp.max(el2, axis=-1, keepdims=True)
    i2 = jnp.min(jnp.where(el2 == m2, lane, big), axis=-1, keepdims=True)
    r21 = jnp.exp(m2 - m1)
    w1 = gval / (1.0 + r21)
    w2 = w1 * r21

    @pl.when(pl.program_id(0) == 0)
    def _():
        run_ref[...] = jnp.zeros_like(run_ref)

    oh1 = jnp.where(lane == i1, 1.0, 0.0)
    oh2 = jnp.where(lane == i2, 1.0, 0.0)
    ri = lax.broadcasted_iota(jnp.int32, (tm, tm), 0)
    ci = lax.broadcasted_iota(jnp.int32, (tm, tm), 1)
    before = jnp.where(ri > ci, 1.0, 0.0).astype(BF)
    base = run_ref[0:1, :]
    tot1 = jnp.sum(oh1, axis=0, keepdims=True)
    tot2 = jnp.sum(oh2, axis=0, keepdims=True)
    c1 = _dot(before, oh1.astype(BF)) + base
    c2 = _dot(before, oh2.astype(BF)) + (base + tot1)
    r1 = jnp.sum(jnp.where(lane == i1, c1, 0.0), axis=-1, keepdims=True)
    r2 = jnp.sum(jnp.where(lane == i2, c2, 0.0), axis=-1, keepdims=True)
    total = base + tot1 + tot2
    run_ref[0:1, :] = total
    cnt_ref[...] = jnp.broadcast_to(total, cnt_ref.shape)

    rt = jnp.zeros((tm, LANES), F32)
    for k, v in enumerate((i1.astype(F32), i2.astype(F32), w1, w2, r1, r2)):
        rt = jnp.where(lane == k, v, rt)
    rt_ref[...] = rt


def _mix(m, x, wo, g1, b1, wr, rb, layer):
    const = lambda shape: pl.BlockSpec((None,) + shape, lambda i: (layer,) + (0,) * len(shape),
                                       pipeline_mode=pl.Buffered(1))
    vec = lambda w: pl.BlockSpec((None, 1, w), lambda i: (layer, 0, 0))
    return pl.pallas_call(
        _mix_kernel,
        grid=(NTOK // MIX_TM,),
        in_specs=[
            pl.BlockSpec((MIX_TM, D_MODEL), lambda i: (i, 0)),
            pl.BlockSpec((MIX_TM, D_MODEL), lambda i: (i, 0)),
            const((D_MODEL, D_MODEL)),
            vec(D_MODEL), vec(D_MODEL),
            const((D_MODEL, LANES)), vec(LANES),
        ],
        out_specs=[
            pl.BlockSpec((MIX_TM, D_MODEL), lambda i: (i, 0)),
            pl.BlockSpec((MIX_TM, LANES), lambda i: (i, 0)),
            pl.BlockSpec((SUBLANES, LANES), lambda i: (i, 0)),
        ],
        out_shape=[
            jax.ShapeDtypeStruct((NTOK, D_MODEL), F32),
            jax.ShapeDtypeStruct((NTOK, LANES), F32),
            jax.ShapeDtypeStruct((NTOK // MIX_TM * SUBLANES, LANES), F32),
        ],
        scratch_shapes=[pltpu.VMEM((D_MODEL, D_MODEL), BF), pltpu.VMEM((SUBLANES, LANES), F32)],
        compiler_params=_cparams(("arbitrary",)),
        name="mix",
    )(m, x, wo, g1, b1, wr, rb)


def _route_tables(rt, cnt):
    e = jnp.concatenate([rt[:, 0], rt[:, 1]]).astype(jnp.int32)
    rank = jnp.concatenate([rt[:, 4], rt[:, 5]]).astype(jnp.int32)
    counts = cnt[-1, :N_EXPERTS].astype(jnp.int32)
    padded = ((counts + MOE_T - 1) // MOE_T) * MOE_T
    ends = jnp.cumsum(padded)
    off = ends - padded
    pos = (jnp.take(off, e) + rank).astype(jnp.int32)
    n_used = (ends[-1] // MOE_T).astype(jnp.int32)
    tiles = jnp.arange(MOE_TILES, dtype=jnp.int32)
    tile_blk = jnp.minimum(tiles, n_used - 1)
    tile_e = jnp.sum((ends[None, :] // MOE_T <= tile_blk[:, None]).astype(jnp.int32), axis=1)
    tile_e = jnp.minimum(tile_e, N_EXPERTS - 1)
    later = (tile_e[None, :] > tile_e[:, None]) & (tiles[None, :] < n_used)
    tile_next = jnp.min(jnp.where(later, tile_e[None, :], N_EXPERTS), axis=1)
    tile_next = jnp.where(tile_next == N_EXPERTS, -1, tile_next).astype(jnp.int32)
    pad_lo = (ends - MOE_T).astype(jnp.int32)
    return pos, tile_blk, tile_e, tile_next, n_used.reshape(1), pad_lo, counts.astype(jnp.int32)


def _row_copy(src, dst, si, di, sem):
    return pltpu.make_async_copy(src.at[pl.ds(si, 1)], dst.at[pl.ds(di, 1)], sem)


def _dispatch_kernel(pos_ref, padlo_ref, cnt_ref, x_ref, xs_hbm, zero_ref, sem):
    i = pl.program_id(0)

    @pl.when(i == 0)
    def _():
        zero_ref[...] = jnp.zeros_like(zero_ref)

        def zero_copy(e):
            lo = pl.multiple_of(padlo_ref[e], MOE_T)
            return pltpu.make_async_copy(zero_ref, xs_hbm.at[pl.ds(lo, MOE_T)], sem)

        def zstart(e, c):
            @pl.when(cnt_ref[e] > 0)
            def _():
                zero_copy(e).start()
            return c

        def zwait(e, c):
            @pl.when(cnt_ref[e] > 0)
            def _():
                zero_copy(e).wait()
            return c

        lax.fori_loop(0, N_EXPERTS, zstart, 0)
        lax.fori_loop(0, N_EXPERTS, zwait, 0)

    t0 = i * DISP_TD

    def start(j, c):
        t = t0 + j
        _row_copy(x_ref, xs_hbm, j, pos_ref[t], sem).start(priority=0)
        _row_copy(x_ref, xs_hbm, j, pos_ref[NTOK + t], sem).start(priority=1)
        return c

    lax.fori_loop(0, DISP_TD, start, 0, unroll=8)
    for _ in range(2):
        pltpu.make_async_copy(x_ref, xs_hbm.at[pl.ds(0, DISP_TD)], sem).wait()


def _dispatch(pos, pad_lo, counts, x1):
    return pl.pallas_call(
        _dispatch_kernel,
        grid_spec=pltpu.PrefetchScalarGridSpec(
            num_scalar_prefetch=3,
            grid=(NTOK // DISP_TD,),
            in_specs=[pl.BlockSpec((DISP_TD, D_MODEL), lambda i, pos, lo, cnt: (i, 0))],
            out_specs=pl.BlockSpec(memory_space=pl.ANY),
            scratch_shapes=[pltpu.VMEM((MOE_T, D_MODEL), F32), pltpu.SemaphoreType.DMA(())],
        ),
        out_shape=jax.ShapeDtypeStruct((MOE_ROWS, D_MODEL), F32),
        compiler_params=_cparams(("arbitrary",)),
        name="moe_dispatch",
    )(pos, pad_lo, counts, x1)


def _expert_kernel(blk_ref, te_ref, nxt_ref, nu_ref, x_ref, wg_hbm, wu_hbm, wd_hbm, o_ref,
                   wg_st, wu_st, wd_st, wg_bf, wu_bf, wd_bf, sem, *, layer):
    i = pl.program_id(0)
    e = te_ref[i]
    first = (i == 0) | (e != te_ref[jnp.maximum(i - 1, 0)])

    def fetch(ex):
        return (pltpu.make_async_copy(wg_hbm.at[layer, ex], wg_st, sem.at[0]),
                pltpu.make_async_copy(wu_hbm.at[layer, ex], wu_st, sem.at[1]),
                pltpu.make_async_copy(wd_hbm.at[layer, ex], wd_st, sem.at[2]))

    @pl.when(i == 0)
    def _():
        for c in fetch(e):
            c.start()

    @pl.when(first)
    def _():
        for c in fetch(e):
            c.wait()
        wg_bf[...] = wg_st[...].astype(BF)
        wu_bf[...] = wu_st[...].astype(BF)
        wd_bf[...] = wd_st[...].astype(BF)
        nxt = nxt_ref[i]

        @pl.when(nxt >= 0)
        def _():
            for c in fetch(nxt):
                c.start()

    @pl.when(i < nu_ref[0])
    def _():
        x = x_ref[...].astype(BF)
        h = _silu(_dot(x, wg_bf[...])) * _dot(x, wu_bf[...])
        o_ref[...] = _dot(h.astype(BF), wd_bf[...])


def _experts(tile_blk, tile_e, tile_next, n_used, xs, wg, wu, wd, layer):
    hbm = pl.BlockSpec(memory_space=pl.ANY)
    return pl.pallas_call(
        functools.partial(_expert_kernel, layer=layer),
        grid_spec=pltpu.PrefetchScalarGridSpec(
            num_scalar_prefetch=4,
            grid=(MOE_TILES,),
            in_specs=[pl.BlockSpec((MOE_T, D_MODEL), lambda i, blk, te, nxt, nu: (blk[i], 0)), hbm, hbm, hbm],
            out_specs=pl.BlockSpec((MOE_T, D_MODEL), lambda i, blk, te, nxt, nu: (blk[i], 0)),
            scratch_shapes=[
                pltpu.VMEM((D_MODEL, D_EXPERT), F32),
                pltpu.VMEM((D_MODEL, D_EXPERT), F32),
                pltpu.VMEM((D_EXPERT, D_MODEL), F32),
                pltpu.VMEM((D_MODEL, D_EXPERT), BF),
                pltpu.VMEM((D_MODEL, D_EXPERT), BF),
                pltpu.VMEM((D_EXPERT, D_MODEL), BF),
                pltpu.SemaphoreType.DMA((3,)),
            ],
        ),
        out_shape=jax.ShapeDtypeStruct((MOE_ROWS, D_MODEL), F32),
        compiler_params=_cparams(("arbitrary",)),
        name="moe_experts",
    )(tile_blk, tile_e, tile_next, n_used, xs, wg, wu, wd)


def _combine_kernel(pos_ref, ys_hbm, x1_ref, rt_ref, g2_ref, b2_ref, x2_ref, xbf_ref, buf_ref, sem):
    i = pl.program_id(0)

    def gather(tile, par):
        def start(j, c):
            t = tile * COMB_TC + j
            _row_copy(ys_hbm, buf_ref.at[par, 0], pos_ref[t], j, sem.at[par]).start(priority=0)
            _row_copy(ys_hbm, buf_ref.at[par, 1], pos_ref[NTOK + t], j, sem.at[par]).start(priority=1)
            return c

        lax.fori_loop(0, COMB_TC, start, 0, unroll=8)

    @pl.when(i == 0)
    def _():
        gather(0, 0)

    @pl.when(i + 1 < pl.num_programs(0))
    def _():
        gather(i + 1, (i + 1) % 2)

    par = i % 2
    for slot in range(2):
        pltpu.make_async_copy(ys_hbm.at[pl.ds(0, COMB_TC)], buf_ref.at[par, slot], sem.at[par]).wait()
    rt = rt_ref[...]
    f = rt[:, 2:3] * buf_ref[par, 0] + rt[:, 3:4] * buf_ref[par, 1]
    x2 = _layer_norm(DN_ALPHA * x1_ref[...] + f, g2_ref[...], b2_ref[...])
    x2_ref[...] = x2
    xbf_ref[...] = x2.astype(BF)


def _combine(pos, ys, x1, rt, g2, b2, layer):
    vec = lambda w: pl.BlockSpec((None, 1, w), lambda i, pos: (layer, 0, 0))
    return pl.pallas_call(
        _combine_kernel,
        grid_spec=pltpu.PrefetchScalarGridSpec(
            num_scalar_prefetch=1,
            grid=(NTOK // COMB_TC,),
            in_specs=[
                pl.BlockSpec(memory_space=pl.ANY),
                pl.BlockSpec((COMB_TC, D_MODEL), lambda i, pos: (i, 0)),
                pl.BlockSpec((COMB_TC, LANES), lambda i, pos: (i, 0)),
                vec(D_MODEL), vec(D_MODEL),
            ],
            out_specs=[
                pl.BlockSpec((COMB_TC, D_MODEL), lambda i, pos: (i, 0)),
                pl.BlockSpec((COMB_TC, D_MODEL), lambda i, pos: (i, 0)),
            ],
            scratch_shapes=[pltpu.VMEM((2, 2, COMB_TC, D_MODEL), F32), pltpu.SemaphoreType.DMA((2,))],
        ),
        out_shape=[
            jax.ShapeDtypeStruct((NTOK, D_MODEL), F32),
            jax.ShapeDtypeStruct((NTOK, D_MODEL), BF),
        ],
        compiler_params=_cparams(("arbitrary",)),
        name="moe_combine",
    )(pos, ys, x1, rt, g2, b2)


def kernel(x_prompt, x_sample, state_ssm, state_conv, state_pool, w_in, pool_map_w, pool_map_b, pool_scale,
           conv_w, conv_b, dt_bias, a_log, d_skip, ssd_norm_w, gate_a_bias, gate_b_bias, w_pool_out, w_ssd_out,
           w_o, ln1_g, ln1_b, router_group_w, router_group_b, router_expert_w, router_expert_b, expert_w_gate,
           expert_w_up, expert_w_down, ln2_g, ln2_b):
    w_in_t = jnp.swapaxes(w_in, 1, 2)
    r_pad = jnp.zeros((DEPTH, D_MODEL, LANES - N_EXPERTS - N_EXPERT_GROUPS), F32)
    wr = jnp.concatenate([router_expert_w, router_group_w, r_pad], axis=-1)
    rb = jnp.concatenate([router_expert_b, router_group_b,
                          jnp.zeros((DEPTH, LANES - N_EXPERTS - N_EXPERT_GROUPS), F32)], axis=-1)[:, None, :]
    row = lambda v: v[:, None, :]
    head_pad = lambda v: jnp.pad(v, ((0, 0), (0, LANES - SSD_HEADS)))[:, None, :]
    dtb = head_pad(dt_bias)
    alog = head_pad(a_log)
    dsk = row(jnp.repeat(d_skip, SSD_HEAD_DIM, axis=-1))
    e_bf = (jnp.arange(SSD_INNER)[None, :] // SSD_HEAD_DIM == jnp.arange(LANES)[:, None]).astype(BF)
    wg = expert_w_gate.reshape(DEPTH, N_EXPERTS, D_MODEL, D_EXPERT)
    wu = expert_w_up.reshape(DEPTH, N_EXPERTS, D_MODEL, D_EXPERT)
    wd = expert_w_down.reshape(DEPTH, N_EXPERTS, D_EXPERT, D_MODEL)
    cs_pad = jnp.pad(state_conv, ((0, 0), (0, 0), (0, DEC_SEQ - (CONV_W - 1)), (0, 0)))
    cs_pad = cs_pad.reshape(DEPTH, NS, CONV_DIM)
    h0_all = state_ssm.reshape(DEPTH, DEC_BATCH, SSD_INNER, D_STATE)

    x = jnp.concatenate([x_prompt.reshape(NP, D_MODEL), x_sample.reshape(NS, D_MODEL)], axis=0)
    x_bf = x.astype(BF)

    ssm_p, conv_p, pool_p, conv_s, pool_s = [], [], [], [], []
    hs_all = None
    for l in range(DEPTH):
        proj = _inproj(x_bf, w_in_t, l)
        tails = [lax.slice(proj, (b * SEQ + SEQ - POOL_BUF, 0), ((b + 1) * SEQ, COL_GA)) for b in range(BATCH)]
        tails = jnp.stack(tails, 0)
        pool_p.append(tails[:, :, COL_U:COL_U + POOL_DIM])
        conv_p.append(tails[:, POOL_BUF - (CONV_W - 1):, COL_XBC:])
        proj_s = lax.slice(proj, (NP, 0), (NTOK, COL_GA)).reshape(DEC_BATCH, DEC_SEQ, COL_GA)
        u_s = proj_s[:, :, COL_U:COL_U + POOL_DIM]
        pool_s.append(jnp.concatenate([state_pool[l][:, DEC_SEQ:], u_s], axis=1))
        conv_s.append(proj_s[:, DEC_SEQ - (CONV_W - 1):, COL_XBC:])

        pmb = row(pool_map_b.reshape(DEPTH, POOL_DIM))
        a_p = _pool_prompt(proj, pool_map_w, pmb, row(pool_scale), l)
        ext_s = jnp.concatenate([jnp.zeros((DEC_BATCH, 1, POOL_DIM), F32), state_pool[l], u_s], axis=1)
        a_s = _pool_sample(ext_s.reshape(DEC_BATCH * POOL_SLAB, POOL_DIM), pool_map_w, pmb, row(pool_scale), l)
        yn_p, h_p = _ssd_prompt(proj, x_bf, w_in_t, conv_w, row(conv_b), dtb, alog, dsk, row(ssd_norm_w),
                                e_bf, l)
        ssm_p.append(h_p)
        yn_s, hs_all = _ssd_sample(proj, x_bf, w_in_t, cs_pad, h0_all, conv_w, row(conv_b), dtb, alog, dsk,
                                   row(ssd_norm_w), e_bf, hs_all, l)
        m = _merge(a_p, a_s, yn_p, yn_s, proj, w_pool_out, w_ssd_out, row(gate_a_bias), row(gate_b_bias), l)
        x1, rt, cnt = _mix(m, x, w_o, row(ln1_g), row(ln1_b), wr, rb, l)
        pos, tile_blk, tile_e, tile_next, n_used, pad_lo, counts = _route_tables(rt, cnt)
        xs = _dispatch(pos, pad_lo, counts, x1)
        ys = _experts(tile_blk, tile_e, tile_next, n_used, xs, wg, wu, wd, l)
        x, x_bf = _combine(pos, ys, x1, rt, row(ln2_g), row(ln2_b), l)

    y_prompt = x[:NP].reshape(BATCH, SEQ, D_MODEL)
    y_sample = x[NP:].reshape(DEC_BATCH, DEC_SEQ, D_MODEL)
    new_ssm_prompt = jnp.stack(ssm_p, 0).reshape(DEPTH, BATCH, SSD_HEADS, SSD_HEAD_DIM, D_STATE)
    new_ssm_sample = hs_all.reshape(DEPTH, DEC_BATCH, SSD_HEADS, SSD_HEAD_DIM, D_STATE)
    return (y_prompt, y_sample, new_ssm_prompt, jnp.stack(conv_p, 0), jnp.stack(pool_p, 0),
            new_ssm_sample, jnp.stack(conv_s, 0), jnp.stack(pool_s, 0))
```
